```python
import math
import jax, jax.numpy as jnp
from jax import lax
import numpy as np

D_MODEL = 1024
BATCH = 8
SEQ = 4096
DEPTH = 2

GRID_W = 64
N_MIXERS = 2
NA_HEADS = 16
NA_HEAD_DIM = D_MODEL // NA_HEADS
NA_KH = 8
NA_KW = 16
DA_HEADS = 8
DA_HEAD_DIM = D_MODEL // (2 * DA_HEADS)
Q_BLOCK = 128
T5_BUCKETS = 32
T5_MAX_DIST = 128
D_FF = 2816
CONV_WIDTH = 3
PLE_DIM = 256
N_NORMS = 5
N_A_LAYERS = (DEPTH + 1) // 2
N_B_LAYERS = DEPTH // 2
EPS = 1e-6

kernel_name = "hybrid_natten_diffattn_convglu_encoder"


def rms_norm(x, g):
    x32 = x.astype(jnp.float32)
    y = x32 * lax.rsqrt(jnp.mean(x32 * x32, axis=-1, keepdims=True) + EPS)
    return (y * g.astype(jnp.float32)).astype(x.dtype)


def t5_bucket(rel):
    half = T5_BUCKETS // 2
    max_exact = half // 2
    sign_off = jnp.where(rel > 0, half, 0)
    n = jnp.abs(rel)
    nf = jnp.maximum(n, 1).astype(jnp.float32)
    large = max_exact + (jnp.log(nf / max_exact) / math.log(T5_MAX_DIST / max_exact)
                         * (half - max_exact)).astype(jnp.int32)
    large = jnp.minimum(large, half - 1)
    return sign_off + jnp.where(n < max_exact, n, large)


def neighborhood_attention(h, w_qkv, rpb, w_o):
    B, S, D = h.shape
    rows = S // GRID_W
    kh = min(NA_KH, rows)
    qkv = (h @ w_qkv).reshape(B, rows, GRID_W, 3, NA_HEADS, NA_HEAD_DIM)
    qkv = qkv.transpose(3, 0, 4, 1, 2, 5)
    q = qkv[0] * (NA_HEAD_DIM ** -0.5)
    k = qkv[1]
    v = qkv[2]
    cols = jnp.arange(GRID_W)
    col_idx = jnp.clip(cols - NA_KW // 2, 0, GRID_W - NA_KW)[:, None] + jnp.arange(NA_KW)[None, :]
    col_off = col_idx - cols[:, None] + (NA_KW - 1)
    rpb_c = rpb[:, :, col_off]

    def row_fn(r):
        r0 = jnp.clip(r - kh // 2, 0, rows - kh)
        k_rows = lax.dynamic_slice_in_dim(k, r0, kh, axis=2)
        v_rows = lax.dynamic_slice_in_dim(v, r0, kh, axis=2)
        k_win = k_rows[:, :, :, col_idx]
        v_win = v_rows[:, :, :, col_idx]
        q_r = lax.dynamic_index_in_dim(q, r, axis=2, keepdims=False)
        s = jnp.einsum('bhcd,bhicjd->bhcij', q_r, k_win).astype(jnp.float32)
        row_off = r0 + jnp.arange(kh) - r + (NA_KH - 1)
        bias = rpb_c[:, row_off].transpose(0, 2, 1, 3)
        s = s + bias[None].astype(jnp.float32)
        a = jax.nn.softmax(s.reshape(B, NA_HEADS, GRID_W, kh * NA_KW), axis=-1).reshape(s.shape)
        return jnp.einsum('bhcij,bhicjd->bhcd', a.astype(v.dtype), v_win)

    o = lax.map(row_fn, jnp.arange(rows))
    o = o.transpose(1, 0, 3, 2, 4).reshape(B, S, D)
    return o @ w_o


def diff_attention(h, w_qkv, lam, subln_g, w_o, t5_table, lambda_init):
    B, S, D = h.shape
    nb = S // Q_BLOCK
    q, k, v = jnp.split(h @ w_qkv, 3, axis=-1)
    q = q.reshape(B, S, DA_HEADS, 2, DA_HEAD_DIM).transpose(0, 2, 3, 1, 4) * (DA_HEAD_DIM ** -0.5)
    k = k.reshape(B, S, DA_HEADS, 2, DA_HEAD_DIM).transpose(0, 2, 3, 1, 4)
    v = v.reshape(B, S, DA_HEADS, 2 * DA_HEAD_DIM).transpose(0, 2, 1, 3)
    lam32 = lam.astype(jnp.float32)
    lam_full = (jnp.exp(jnp.sum(lam32[0] * lam32[1])) - jnp.exp(jnp.sum(lam32[2] * lam32[3]))
                + lambda_init)
    q_blocks = q.reshape(B, DA_HEADS, 2, nb, Q_BLOCK, DA_HEAD_DIM).transpose(3, 0, 1, 2, 4, 5)
    kpos = jnp.arange(S)

    def block_fn(args):
        qb, i = args
        qpos = i * Q_BLOCK + jnp.arange(Q_BLOCK)
        bias = t5_table[t5_bucket(kpos[None, :] - qpos[:, None])]
        bias = bias.transpose(2, 0, 1).astype(jnp.float32)
        s = jnp.einsum('bhmqd,bhmkd->bhmqk', qb, k).astype(jnp.float32) + bias[None, :, None]
        a = jax.nn.softmax(s, axis=-1)
        w = a[:, :, 0] - lam_full * a[:, :, 1]
        return jnp.einsum('bhqk,bhkd->bhqd', w.astype(v.dtype), v)

    o = lax.map(block_fn, (q_blocks, jnp.arange(nb)))
    o = o.transpose(1, 2, 0, 3, 4).reshape(B, DA_HEADS, S, 2 * DA_HEAD_DIM)
    o = rms_norm(o, subln_g) * (1.0 - lambda_init)
    o = o.transpose(0, 2, 1, 3).reshape(B, S, D)
    return o @ w_o


def conv_glu_ffn(h, w_in, conv_w, conv_b, w_out):
    u = h @ w_in
    gate, val = jnp.split(u, 2, axis=-1)
    gate = lax.conv_general_dilated(
        gate, conv_w[:, None, :].astype(gate.dtype), window_strides=(1,),
        padding=((CONV_WIDTH // 2, CONV_WIDTH // 2),),
        dimension_numbers=('NWC', 'WIO', 'NWC'), feature_group_count=D_FF) + conv_b
    return (jax.nn.gelu(gate, approximate=True) * val) @ w_out


def setup_inputs(seed: int = 0) -> dict:
    key = jax.random.key(seed)
    ks = jax.random.split(key, 20)
    f32 = jnp.float32
    D, F = D_MODEL, D_FF
    nrm = lambda k, s: jax.random.normal(k, s, f32)
    return {
        'x': nrm(ks[0], (BATCH, SEQ, D)),
        'p': nrm(ks[1], (DEPTH, BATCH, SEQ, PLE_DIM)),
        'norm_g': 1.0 + 0.05 * nrm(ks[2], (DEPTH, N_NORMS, D)),
        'na_w_qkv': nrm(ks[3], (N_A_LAYERS, D, 3 * D)) * D ** -0.5,
        'na_rpb': 0.1 * nrm(ks[4], (N_A_LAYERS, NA_HEADS, 2 * NA_KH - 1, 2 * NA_KW - 1)),
        'na_w_o': nrm(ks[5], (N_A_LAYERS, D, D)) * D ** -0.5,
        'da_w_qkv': nrm(ks[6], (N_B_LAYERS, D, 3 * D)) * D ** -0.5,
        'da_lambda': 0.1 * nrm(ks[7], (N_B_LAYERS, 4, DA_HEAD_DIM)),
        'da_subln_g': 1.0 + 0.05 * nrm(ks[8], (N_B_LAYERS, 2 * DA_HEAD_DIM)),
        'da_w_o': nrm(ks[9], (N_B_LAYERS, D, D)) * D ** -0.5,
        't5_table': 0.1 * nrm(ks[10], (T5_BUCKETS, DA_HEADS)),
        'ffn_w_in': nrm(ks[11], (DEPTH, D, 2 * F)) * D ** -0.5,
        'ffn_conv_w': nrm(ks[12], (DEPTH, CONV_WIDTH, F)) * CONV_WIDTH ** -0.5,
        'ffn_conv_b': 0.02 * nrm(ks[13], (DEPTH, F)),
        'ffn_w_out': nrm(ks[14], (DEPTH, F, D)) * F ** -0.5,
        'ple_w_gate': nrm(ks[15], (DEPTH, D, D)) * D ** -0.5,
        'ple_w_proj': nrm(ks[16], (DEPTH, PLE_DIM, D)) * PLE_DIM ** -0.5,
    }


def reference(x, p, norm_g, na_w_qkv, na_rpb, na_w_o, da_w_qkv, da_lambda, da_subln_g,
              da_w_o, t5_table, ffn_w_in, ffn_conv_w, ffn_conv_b, ffn_w_out,
              ple_w_gate, ple_w_proj):
    for i in range(DEPTH):
        g = norm_g[i]
        h = rms_norm(x, g[0])
        j = i // N_MIXERS
        if i % N_MIXERS == 0:
            m = neighborhood_attention(h, na_w_qkv[j], na_rpb[j], na_w_o[j])
        else:
            lambda_init = 0.8 - 0.6 * math.exp(-0.3 * i)
            m = diff_attention(h, da_w_qkv[j], da_lambda[j], da_subln_g[j], da_w_o[j],
                               t5_table, lambda_init)
        x = x + rms_norm(m, g[1])
        f = conv_glu_ffn(rms_norm(x, g[2]), ffn_w_in[i], ffn_conv_w[i], ffn_conv_b[i], ffn_w_out[i])
        x = x + rms_norm(f, g[3])
        gate = jax.nn.sigmoid(rms_norm(x, g[4]) @ ple_w_gate[i])
        x = x + gate * (p[i] @ ple_w_proj[i])
    return x
```

```python
import functools
import math

import jax
import jax.numpy as jnp
from jax import lax
from jax.experimental import pallas as pl
from jax.experimental.pallas import tpu as pltpu

F32 = jnp.float32
BF16 = jnp.bfloat16

GRID_W = 64
NA_HEADS = 16
NA_KH = 8
NA_KW = 16
DA_HEADS = 8
T5_BUCKETS = 32
T5_MAX_DIST = 128
EPS = 1e-6
N_MIXERS = 2

V7X_LANES = 128
V7X_VMEM_LIMIT_BYTES = 56 * 1024 * 1024

NEG_BIG = -1e30

ROW_TILE = 512
CONV_HALO = 16
FFN_CHUNK = 256
DA_TQ = 256
DA_TK = 512


def _rms(x, g):
    ms = jnp.mean(x * x, axis=-1, keepdims=True)
    return x * lax.rsqrt(ms + EPS) * g


def _params(*sem):
    return pltpu.CompilerParams(dimension_semantics=sem,
                                vmem_limit_bytes=V7X_VMEM_LIMIT_BYTES)


def _resident(shape):
    nd = len(shape)
    return pl.BlockSpec(shape, lambda *_: (0,) * nd)


def _norm_matmul_kernel(x_ref, g_ref, w_ref, o_ref, *, col_chunk):
    h = _rms(x_ref[...], g_ref[...]).astype(BF16)
    for c in range(o_ref.shape[1] // col_chunk):
        sl = slice(c * col_chunk, (c + 1) * col_chunk)
        o_ref[:, sl] = jnp.dot(h, w_ref[:, sl],
                               preferred_element_type=F32).astype(o_ref.dtype)


def _norm_matmul(x2d, g, w):
    n, d = x2d.shape
    nout = w.shape[1]
    return pl.pallas_call(
        functools.partial(_norm_matmul_kernel, col_chunk=1024),
        grid=(n // ROW_TILE,),
        in_specs=[pl.BlockSpec((ROW_TILE, d), lambda i: (i, 0)),
                  _resident((1, d)),
                  _resident((d, nout))],
        out_specs=pl.BlockSpec((ROW_TILE, nout), lambda i: (i, 0)),
        out_shape=jax.ShapeDtypeStruct((n, nout), BF16),
        compiler_params=_params("parallel"),
        name="norm_matmul",
    )(x2d, g, w)


def _matmul_norm_res_kernel(a_ref, w_ref, g_ref, x_ref, o_ref):
    f = jnp.dot(a_ref[...], w_ref[...], preferred_element_type=F32)
    o_ref[...] = x_ref[...] + _rms(f, g_ref[...])


def _matmul_norm_res(a2d, w, g, x2d):
    n, d = x2d.shape
    k = a2d.shape[1]
    return pl.pallas_call(
        _matmul_norm_res_kernel,
        grid=(n // ROW_TILE,),
        in_specs=[pl.BlockSpec((ROW_TILE, k), lambda i: (i, 0)),
                  _resident((k, d)),
                  _resident((1, d)),
                  pl.BlockSpec((ROW_TILE, d), lambda i: (i, 0))],
        out_specs=pl.BlockSpec((ROW_TILE, d), lambda i: (i, 0)),
        out_shape=jax.ShapeDtypeStruct((n, d), F32),
        compiler_params=_params("parallel"),
        name="matmul_norm_res",
    )(a2d, w, g, x2d)


def _gelu_tanh(x):
    c = math.sqrt(2.0 / math.pi)
    return 0.5 * x * (1.0 + jnp.tanh(c * (x + 0.044715 * (x * x * x))))


def _ffn_in_kernel(xp_ref, x_ref, xn_ref, g_ref, wg_ref, wv_ref, cw_ref, cb_ref,
                   o_ref, gext_ref, *, tiles_per_seq):
    i = pl.program_id(0)
    t = i % tiles_per_seq
    g = g_ref[...]
    has_prev = (t > 0).astype(F32)
    has_next = (t < tiles_per_seq - 1).astype(F32)
    h_prev = (_rms(xp_ref[...], g) * has_prev).astype(BF16)
    h_main = _rms(x_ref[...], g).astype(BF16)
    h_next = (_rms(xn_ref[...], g) * has_next).astype(BF16)
    h_ext = jnp.concatenate([h_prev, h_main, h_next], axis=0)
    tm = x_ref.shape[0]
    nf = o_ref.shape[1]
    for c in range(nf // FFN_CHUNK):
        sl = slice(c * FFN_CHUNK, (c + 1) * FFN_CHUNK)
        gext_ref[...] = jnp.dot(h_ext, wg_ref[:, sl], preferred_element_type=F32)
        val = jnp.dot(h_main, wv_ref[:, sl], preferred_element_type=F32)
        cw = cw_ref[:, sl]
        gate = (gext_ref[CONV_HALO - 1:CONV_HALO - 1 + tm, :] * cw[0:1]
                + gext_ref[CONV_HALO:CONV_HALO + tm, :] * cw[1:2]
                + gext_ref[CONV_HALO + 1:CONV_HALO + 1 + tm, :] * cw[2:3]
                + cb_ref[:, sl])
        o_ref[:, sl] = (_gelu_tanh(gate) * val).astype(o_ref.dtype)


def _ffn_in(x2d, g, wg, wv, conv_w, conv_b, seq_len):
    n, d = x2d.shape
    nf = wg.shape[1]
    tm = ROW_TILE
    hb = tm // CONV_HALO
    last = n // CONV_HALO - 1
    return pl.pallas_call(
        functools.partial(_ffn_in_kernel, tiles_per_seq=seq_len // tm),
        grid=(n // tm,),
        in_specs=[
            pl.BlockSpec((CONV_HALO, d), lambda i: (jnp.maximum(i * hb - 1, 0), 0)),
            pl.BlockSpec((tm, d), lambda i: (i, 0)),
            pl.BlockSpec((CONV_HALO, d), lambda i: (jnp.minimum((i + 1) * hb, last), 0)),
            _resident((1, d)),
            _resident((d, nf)),
            _resident((d, nf)),
            _resident((3, nf)),
            _resident((1, nf)),
        ],
        out_specs=pl.BlockSpec((tm, nf), lambda i: (i, 0)),
        out_shape=jax.ShapeDtypeStruct((n, nf), BF16),
        scratch_shapes=[pltpu.VMEM((tm + 2 * CONV_HALO, FFN_CHUNK), F32)],
        compiler_params=_params("parallel"),
        name="ffn_in",
    )(x2d, x2d, x2d, g, wg, wv, conv_w, conv_b)


def _ffn_out_ple_kernel(a_ref, wo_ref, g3_ref, x_ref, g4_ref, wgate_ref, p_ref,
                        wproj_ref, o_ref):
    f = jnp.dot(a_ref[...], wo_ref[...], preferred_element_type=F32)
    x2 = x_ref[...] + _rms(f, g3_ref[...])
    hg = _rms(x2, g4_ref[...]).astype(BF16)
    gate = jax.nn.sigmoid(jnp.dot(hg, wgate_ref[...], preferred_element_type=F32))
    emb = jnp.dot(p_ref[...].astype(BF16), wproj_ref[...], preferred_element_type=F32)
    o_ref[...] = x2 + gate * emb


def _ffn_out_ple(a2d, w_out, g3, x2d, g4, w_gate, p2d, w_proj):
    n, d = x2d.shape
    nf = a2d.shape[1]
    pd = p2d.shape[1]
    tm = ROW_TILE
    return pl.pallas_call(
        _ffn_out_ple_kernel,
        grid=(n // tm,),
        in_specs=[pl.BlockSpec((tm, nf), lambda i: (i, 0)),
                  _resident((nf, d)),
                  _resident((1, d)),
                  pl.BlockSpec((tm, d), lambda i: (i, 0)),
                  _resident((1, d)),
                  _resident((d, d)),
                  pl.BlockSpec((tm, pd), lambda i: (i, 0)),
                  _resident((pd, d))],
        out_specs=pl.BlockSpec((tm, d), lambda i: (i, 0)),
        out_shape=jax.ShapeDtypeStruct((n, d), F32),
        compiler_params=_params("parallel"),
        name="ffn_out_ple",
    )(a2d, w_out, g3, x2d, g4, w_gate, p2d, w_proj)


def _na_kernel(q_ref, k_ref, v_ref, bias_ref, o_ref, *, rows):
    w = GRID_W
    nkeys = NA_KH * w
    lane = lax.broadcasted_iota(jnp.int32, (w, V7X_LANES), 1)
    lo = lane < (V7X_LANES // 2)
    scale = jnp.asarray((V7X_LANES // 2) ** -0.5, BF16)

    def row_fn(r, carry):
        r0 = jnp.clip(r - NA_KH // 2, 0, rows - NA_KH)
        d = r - r0
        q = q_ref[0, pl.ds(pl.multiple_of(r * w, w), w), :] * scale
        zero = jnp.zeros_like(q)
        qs = jnp.concatenate([jnp.where(lo, q, zero), jnp.where(lo, zero, q)], axis=0)
        koff = pl.multiple_of(r0 * w, w)
        kk = k_ref[0, pl.ds(koff, nkeys), :]
        vv = v_ref[0, pl.ds(koff, nkeys), :]
        s = lax.dot_general(qs, kk, (((1,), (1,)), ((), ())),
                            preferred_element_type=F32)
        s = s + jnp.concatenate([bias_ref[0, 0, d], bias_ref[0, 1, d]], axis=0)
        m = jnp.max(s, axis=-1, keepdims=True)
        p = jnp.exp(s - m)
        l = jnp.sum(p, axis=-1, keepdims=True)
        o = jnp.dot(p.astype(BF16), vv, preferred_element_type=F32) / l
        out = jnp.where(lo, o[:w], o[w:])
        o_ref[0, pl.ds(pl.multiple_of(r * w, w), w), :] = out.astype(o_ref.dtype)
        return carry

    lax.fori_loop(0, rows, row_fn, 0)


def _na_bias_table(rpb):
    h = rpb.shape[0]
    w = GRID_W
    d = jnp.arange(NA_KH)
    i = jnp.arange(NA_KH)
    row_off = i[None, :] - d[:, None] + (NA_KH - 1)
    c = jnp.arange(w)
    kc = jnp.arange(w)
    c0 = jnp.clip(c - NA_KW // 2, 0, w - NA_KW)
    valid = (kc[None, :] >= c0[:, None]) & (kc[None, :] < c0[:, None] + NA_KW)
    col_off = jnp.clip(kc[None, :] - c[:, None] + (NA_KW - 1), 0, 2 * NA_KW - 2)
    b = rpb[:, row_off[:, None, :, None], col_off[None, :, None, :]]
    b = jnp.where(valid[None, None, :, None, :], b.astype(F32), NEG_BIG)
    return b.reshape(h // 2, 2, NA_KH, w, NA_KH * w)


def _neighborhood_attention(qkv, bias, batch, seq, d_model):
    hp = NA_HEADS // 2
    rows = seq // GRID_W
    blk = (1, seq, V7X_LANES)
    return pl.pallas_call(
        functools.partial(_na_kernel, rows=rows),
        grid=(hp, batch),
        in_specs=[pl.BlockSpec(blk, lambda h, b: (b, 0, h)),
                  pl.BlockSpec(blk, lambda h, b: (b, 0, hp + h)),
                  pl.BlockSpec(blk, lambda h, b: (b, 0, 2 * hp + h)),
                  pl.BlockSpec((1,) + bias.shape[1:], lambda h, b: (h, 0, 0, 0, 0))],
        out_specs=pl.BlockSpec(blk, lambda h, b: (b, 0, h)),
        out_shape=jax.ShapeDtypeStruct((batch, seq, d_model), BF16),
        compiler_params=_params("parallel", "parallel"),
        name="neighborhood_attention",
    )(qkv, qkv, qkv, bias)


def _t5_bucket(rel):
    half = T5_BUCKETS // 2
    max_exact = half // 2
    sign_off = jnp.where(rel > 0, half, 0)
    n = jnp.abs(rel)
    nf = jnp.maximum(n, 1).astype(F32)
    large = max_exact + (jnp.log(nf / max_exact) / math.log(T5_MAX_DIST / max_exact)
                         * (half - max_exact)).astype(jnp.int32)
    large = jnp.minimum(large, half - 1)
    return sign_off + jnp.where(n < max_exact, n, large)


_DA_NEAR_OFFSETS = (-DA_TK, -DA_TQ, 0, DA_TQ)


def _da_bias_tables(t5_table):
    qq = jnp.arange(DA_TQ)[:, None]
    kk = jnp.arange(DA_TK)[None, :]
    tiles = []
    for off in _DA_NEAR_OFFSETS:
        tiles.append(t5_table[_t5_bucket(off + kk - qq)])
    near = jnp.stack(tiles, axis=0).transpose(3, 0, 1, 2).astype(F32)
    far = jnp.stack([t5_table[_t5_bucket(jnp.asarray(-T5_MAX_DIST))],
                     t5_table[_t5_bucket(jnp.asarray(T5_MAX_DIST))]], axis=1)
    return near, far.astype(F32)


def _da_kernel(far_ref, q_ref, k_ref, v_ref, near_ref, lam_ref, g_ref, o_ref,
               qs_ref, m_ref, l_ref, acc_ref, *, lambda_init, n_kt):
    h = pl.program_id(0)
    qi = pl.program_id(2)
    tq, tk = DA_TQ, DA_TK
    half = V7X_LANES // 2
    reps = tk // V7X_LANES

    lane = lax.broadcasted_iota(jnp.int32, (tq, V7X_LANES), 1)
    lo = lane < half
    q = q_ref[0] * jnp.asarray(half ** -0.5, BF16)
    zero = jnp.zeros_like(q)
    qs_ref[0:tq, :] = jnp.where(lo, q, zero)
    qs_ref[tq:2 * tq, :] = jnp.where(lo, zero, q)
    m_ref[...] = jnp.full(m_ref.shape, NEG_BIG, F32)
    l_ref[...] = jnp.zeros(l_ref.shape, F32)
    acc_ref[...] = jnp.zeros(acc_ref.shape, F32)

    def tile(j, bias):
        off = pl.multiple_of(j * tk, tk)
        kk = k_ref[0, pl.ds(off, tk), :]
        vv = v_ref[0, pl.ds(off, tk), :]
        s = lax.dot_general(qs_ref[...], kk, (((1,), (1,)), ((), ())),
                            preferred_element_type=F32)
        if bias.ndim == 0:
            s = s + bias
        else:
            s = (s.reshape(2, tq, tk) + bias[None]).reshape(2 * tq, tk)
        m_prev = m_ref[...]
        m_next = jnp.maximum(m_prev, jnp.max(s, axis=-1, keepdims=True))
        alpha = jnp.exp(m_prev - m_next)
        p = jnp.exp(s - pltpu.repeat(m_next, reps, axis=1))
        l_ref[...] = alpha * l_ref[...] + jnp.sum(p, axis=-1, keepdims=True)
        acc_ref[...] = alpha * acc_ref[...] + jnp.dot(
            p.astype(BF16), vv, preferred_element_type=F32)
        m_ref[...] = m_next

    jn = (qi + 1) // 2 - 1
    par = qi % 2

    def far_left(j, c):
        tile(j, far_ref[h, 0])
        return c

    def far_right(j, c):
        tile(j, far_ref[h, 1])
        return c

    lax.fori_loop(0, jn, far_left, 0)

    @pl.when(jn >= 0)
    def _():
        tile(jn, near_ref[0, par])

    @pl.when(jn + 1 < n_kt)
    def _():
        tile(jn + 1, near_ref[0, par + 2])

    lax.fori_loop(jn + 2, n_kt, far_right, 0)

    lam = lam_ref[...]
    lam_full = (jnp.exp(jnp.sum(lam[0:1] * lam[1:2], axis=-1, keepdims=True))
                - jnp.exp(jnp.sum(lam[2:3] * lam[3:4], axis=-1, keepdims=True))
                + lambda_init)
    o_all = acc_ref[...] / l_ref[...]
    o = o_all[:tq] - lam_full * o_all[tq:]
    o = _rms(o, g_ref[...]) * (1.0 - lambda_init)
    o_ref[0] = o.astype(o_ref.dtype)


def _diff_attention(qkv, near, far, lam, subln_g, lambda_init, batch, seq, d_model):
    nh = DA_HEADS
    n_qt = seq // DA_TQ
    n_kt = seq // DA_TK
    kv_blk = (1, seq, V7X_LANES)
    q_blk = (1, DA_TQ, V7X_LANES)
    return pl.pallas_call(
        functools.partial(_da_kernel, lambda_init=lambda_init, n_kt=n_kt),
        grid=(nh, batch, n_qt),
        in_specs=[pl.BlockSpec(memory_space=pltpu.SMEM),
                  pl.BlockSpec(q_blk, lambda h, b, i: (b, i, h)),
                  pl.BlockSpec(kv_blk, lambda h, b, i: (b, 0, nh + h)),
                  pl.BlockSpec(kv_blk, lambda h, b, i: (b, 0, 2 * nh + h)),
                  pl.BlockSpec((1,) + near.shape[1:], lambda h, b, i: (h, 0, 0, 0)),
                  _resident(lam.shape),
                  _resident(subln_g.shape)],
        out_specs=pl.BlockSpec(q_blk, lambda h, b, i: (b, i, h)),
        out_shape=jax.ShapeDtypeStruct((batch, seq, d_model), BF16),
        scratch_shapes=[pltpu.VMEM((2 * DA_TQ, V7X_LANES), BF16),
                        pltpu.VMEM((2 * DA_TQ, V7X_LANES), F32),
                        pltpu.VMEM((2 * DA_TQ, V7X_LANES), F32),
                        pltpu.VMEM((2 * DA_TQ, V7X_LANES), F32)],
        compiler_params=_params("parallel", "parallel", "arbitrary"),
        name="diff_attention",
    )(far, qkv, qkv, qkv, near, lam, subln_g)


def kernel(x, p, norm_g, na_w_qkv, na_rpb, na_w_o, da_w_qkv, da_lambda, da_subln_g,
           da_w_o, t5_table, ffn_w_in, ffn_conv_w, ffn_conv_b, ffn_w_out,
           ple_w_gate, ple_w_proj):
    batch, seq, d_model = x.shape
    depth = norm_g.shape[0]
    n_tok = batch * seq
    d_ff = ffn_w_out.shape[1]
    x2d = x.reshape(n_tok, d_model)

    for i in range(depth):
        g = norm_g[i].astype(F32)
        j = i // N_MIXERS
        if i % N_MIXERS == 0:
            qkv = _norm_matmul(x2d, g[0:1], na_w_qkv[j].astype(BF16))
            att = _neighborhood_attention(qkv.reshape(batch, seq, 3 * d_model),
                                          _na_bias_table(na_rpb[j]), batch, seq, d_model)
            w_o = na_w_o[j]
        else:
            lambda_init = 0.8 - 0.6 * math.exp(-0.3 * i)
            qkv = _norm_matmul(x2d, g[0:1], da_w_qkv[j].astype(BF16))
            near, far = _da_bias_tables(t5_table)
            att = _diff_attention(qkv.reshape(batch, seq, 3 * d_model), near, far,
                                  da_lambda[j].astype(F32),
                                  da_subln_g[j].astype(F32)[None, :],
                                  lambda_init, batch, seq, d_model)
            w_o = da_w_o[j]
        x2d = _matmul_norm_res(att.reshape(n_tok, d_model), w_o.astype(BF16), g[1:2], x2d)
        w_in = ffn_w_in[i].astype(BF16)
        act = _ffn_in(x2d, g[2:3], w_in[:, :d_ff], w_in[:, d_ff:],
                      ffn_conv_w[i].astype(F32), ffn_conv_b[i].astype(F32)[None, :], seq)
        x2d = _ffn_out_ple(act, ffn_w_out[i].astype(BF16), g[3:4], x2d, g[4:5],
                           ple_w_gate[i].astype(BF16),
                           p[i].reshape(n_tok, p.shape[-1]),
                           ple_w_proj[i].astype(BF16))
    return x2d.reshape(batch, seq, d_model)
```

```python
import functools
import math

import jax
import jax.numpy as jnp
from jax import lax
from jax.experimental import pallas as pl
from jax.experimental.pallas import tpu as pltpu

F32 = jnp.float32
BF16 = jnp.bfloat16

GRID_W = 64
NA_HEADS = 16
NA_KH = 8
NA_KW = 16
DA_HEADS = 8
T5_BUCKETS = 32
T5_MAX_DIST = 128
EPS = 1e-6
N_MIXERS = 2

V7X_LANES = 128
V7X_VMEM_LIMIT_BYTES = 56 * 1024 * 1024

NEG_BIG = -1e30
LOG2E = math.log2(math.e)

ROW_TILE = 512
CONV_HALO = 16
FFN_CHUNK = 256
DA_TQ = 256
DA_TK = 512


def _rms(x, g):
    ms = jnp.mean(x * x, axis=-1, keepdims=True)
    return x * lax.rsqrt(ms + EPS) * g


def _params(*sem):
    return pltpu.CompilerParams(dimension_semantics=sem,
                                vmem_limit_bytes=V7X_VMEM_LIMIT_BYTES)


def _resident(shape):
    nd = len(shape)
    return pl.BlockSpec(shape, lambda *_: (0,) * nd)


def _toeplitz(v, n_rows, n_cols):
    length = n_rows + n_cols - 1
    lead = v.shape[:-1]
    flat = jnp.tile(v, (1,) * len(lead) + (n_rows + 1,))[..., :n_rows * (length + 1)]
    hankel = flat.reshape(lead + (n_rows, length + 1))[..., :n_cols]
    return hankel[..., ::-1]


def _norm_matmul_kernel(x_ref, g_ref, w_ref, o_ref, *, col_chunk):
    h = _rms(x_ref[...], g_ref[...]).astype(BF16)
    for c in range(o_ref.shape[1] // col_chunk):
        sl = slice(c * col_chunk, (c + 1) * col_chunk)
        o_ref[:, sl] = jnp.dot(h, w_ref[:, sl],
                               preferred_element_type=F32).astype(o_ref.dtype)


def _norm_matmul(x2d, g, w):
    n, d = x2d.shape
    nout = w.shape[1]
    return pl.pallas_call(
        functools.partial(_norm_matmul_kernel, col_chunk=1024),
        grid=(n // ROW_TILE,),
        in_specs=[pl.BlockSpec((ROW_TILE, d), lambda i: (i, 0)),
                  _resident((1, d)),
                  _resident((d, nout))],
        out_specs=pl.BlockSpec((ROW_TILE, nout), lambda i: (i, 0)),
        out_shape=jax.ShapeDtypeStruct((n, nout), BF16),
        compiler_params=_params("parallel"),
        name="norm_matmul",
    )(x2d, g, w)


def _da_proj_kernel(x_ref, g_ref, wqk_ref, wvt_ref, qk_ref, vt_ref, *, col_chunk):
    h = _rms(x_ref[...], g_ref[...]).astype(BF16)
    for c in range(qk_ref.shape[1] // col_chunk):
        sl = slice(c * col_chunk, (c + 1) * col_chunk)
        qk_ref[:, sl] = jnp.dot(h, wqk_ref[:, sl],
                                preferred_element_type=F32).astype(qk_ref.dtype)
    vt_ref[0] = lax.dot_general(wvt_ref[...], h, (((1,), (1,)), ((), ())),
                                preferred_element_type=F32).astype(vt_ref.dtype)


def _da_proj(x2d, g, wqk, wvt, batch, seq):
    n, d = x2d.shape
    tps = seq // ROW_TILE
    return pl.pallas_call(
        functools.partial(_da_proj_kernel, col_chunk=1024),
        grid=(n // ROW_TILE,),
        in_specs=[pl.BlockSpec((ROW_TILE, d), lambda i: (i, 0)),
                  _resident((1, d)),
                  _resident(wqk.shape),
                  _resident(wvt.shape)],
        out_specs=[pl.BlockSpec((ROW_TILE, wqk.shape[1]), lambda i: (i, 0)),
                   pl.BlockSpec((1, wvt.shape[0], ROW_TILE), lambda i: (i // tps, 0, i % tps))],
        out_shape=[jax.ShapeDtypeStruct((n, wqk.shape[1]), BF16),
                   jax.ShapeDtypeStruct((batch, wvt.shape[0], seq), BF16)],
        compiler_params=_params("parallel"),
        name="da_proj",
    )(x2d, g, wqk, wvt)


def _matmul_norm_res_kernel(a_ref, w_ref, g_ref, x_ref, o_ref):
    f = jnp.dot(a_ref[...], w_ref[...], preferred_element_type=F32)
    o_ref[...] = x_ref[...] + _rms(f, g_ref[...])


def _matmul_norm_res(a2d, w, g, x2d):
    n, d = x2d.shape
    k = a2d.shape[1]
    return pl.pallas_call(
        _matmul_norm_res_kernel,
        grid=(n // ROW_TILE,),
        in_specs=[pl.BlockSpec((ROW_TILE, k), lambda i: (i, 0)),
                  _resident((k, d)),
                  _resident((1, d)),
                  pl.BlockSpec((ROW_TILE, d), lambda i: (i, 0))],
        out_specs=pl.BlockSpec((ROW_TILE, d), lambda i: (i, 0)),
        out_shape=jax.ShapeDtypeStruct((n, d), F32),
        compiler_params=_params("parallel"),
        name="matmul_norm_res",
    )(a2d, w, g, x2d)


def _gelu_tanh(x):
    c = math.sqrt(2.0 / math.pi)
    return 0.5 * x * (1.0 + jnp.tanh(c * (x + 0.044715 * (x * x * x))))


def _ffn_in_kernel(xp_ref, x_ref, xn_ref, g_ref, wg_ref, wv_ref, cw_ref, cb_ref,
                   o_ref, gext_ref, *, tiles_per_seq):
    i = pl.program_id(0)
    t = i % tiles_per_seq
    g = g_ref[...]
    has_prev = (t > 0).astype(F32)
    has_next = (t < tiles_per_seq - 1).astype(F32)
    h_prev = (_rms(xp_ref[...], g) * has_prev).astype(BF16)
    h_main = _rms(x_ref[...], g).astype(BF16)
    h_next = (_rms(xn_ref[...], g) * has_next).astype(BF16)
    h_ext = jnp.concatenate([h_prev, h_main, h_next], axis=0)
    tm = x_ref.shape[0]
    nf = o_ref.shape[1]
    for c in range(nf // FFN_CHUNK):
        sl = slice(c * FFN_CHUNK, (c + 1) * FFN_CHUNK)
        gext_ref[...] = jnp.dot(h_ext, wg_ref[:, sl], preferred_element_type=F32)
        val = jnp.dot(h_main, wv_ref[:, sl], preferred_element_type=F32)
        cw = cw_ref[:, sl]
        gate = (gext_ref[CONV_HALO - 1:CONV_HALO - 1 + tm, :] * cw[0:1]
                + gext_ref[CONV_HALO:CONV_HALO + tm, :] * cw[1:2]
                + gext_ref[CONV_HALO + 1:CONV_HALO + 1 + tm, :] * cw[2:3]
                + cb_ref[:, sl])
        o_ref[:, sl] = (_gelu_tanh(gate) * val).astype(o_ref.dtype)


def _ffn_in(x2d, g, wg, wv, conv_w, conv_b, seq_len):
    n, d = x2d.shape
    nf = wg.shape[1]
    tm = ROW_TILE
    hb = tm // CONV_HALO
    last = n // CONV_HALO - 1
    return pl.pallas_call(
        functools.partial(_ffn_in_kernel, tiles_per_seq=seq_len // tm),
        grid=(n // tm,),
        in_specs=[
            pl.BlockSpec((CONV_HALO, d), lambda i: (jnp.maximum(i * hb - 1, 0), 0)),
            pl.BlockSpec((tm, d), lambda i: (i, 0)),
            pl.BlockSpec((CONV_HALO, d), lambda i: (jnp.minimum((i + 1) * hb, last), 0)),
            _resident((1, d)),
            _resident((d, nf)),
            _resident((d, nf)),
            _resident((3, nf)),
            _resident((1, nf)),
        ],
        out_specs=pl.BlockSpec((tm, nf), lambda i: (i, 0)),
        out_shape=jax.ShapeDtypeStruct((n, nf), BF16),
        scratch_shapes=[pltpu.VMEM((tm + 2 * CONV_HALO, FFN_CHUNK), F32)],
        compiler_params=_params("parallel"),
        name="ffn_in",
    )(x2d, x2d, x2d, g, wg, wv, conv_w, conv_b)


def _ffn_out_ple_kernel(a_ref, wo_ref, g3_ref, x_ref, g4_ref, wgate_ref, p_ref,
                        wproj_ref, o_ref):
    f = jnp.dot(a_ref[...], wo_ref[...], preferred_element_type=F32)
    x2 = x_ref[...] + _rms(f, g3_ref[...])
    hg = _rms(x2, g4_ref[...]).astype(BF16)
    gate = jax.nn.sigmoid(jnp.dot(hg, wgate_ref[...], preferred_element_type=F32))
    emb = jnp.dot(p_ref[...].astype(BF16), wproj_ref[...], preferred_element_type=F32)
    o_ref[...] = x2 + gate * emb


def _ffn_out_ple(a2d, w_out, g3, x2d, g4, w_gate, p2d, w_proj):
    n, d = x2d.shape
    nf = a2d.shape[1]
    pd = p2d.shape[1]
    tm = ROW_TILE
    return pl.pallas_call(
        _ffn_out_ple_kernel,
        grid=(n // tm,),
        in_specs=[pl.BlockSpec((tm, nf), lambda i: (i, 0)),
                  _resident((nf, d)),
                  _resident((1, d)),
                  pl.BlockSpec((tm, d), lambda i: (i, 0)),
                  _resident((1, d)),
                  _resident((d, d)),
                  pl.BlockSpec((tm, pd), lambda i: (i, 0)),
                  _resident((pd, d))],
        out_specs=pl.BlockSpec((tm, d), lambda i: (i, 0)),
        out_shape=jax.ShapeDtypeStruct((n, d), F32),
        compiler_params=_params("parallel"),
        name="ffn_out_ple",
    )(a2d, w_out, g3, x2d, g4, w_gate, p2d, w_proj)


def _na_kernel(q_ref, k_ref, v_ref, bias_ref, o_ref, *, rows):
    w = GRID_W
    nkeys = NA_KH * w
    lane = lax.broadcasted_iota(jnp.int32, (w, V7X_LANES), 1)
    lo = lane < (V7X_LANES // 2)
    scale = jnp.asarray((V7X_LANES // 2) ** -0.5, BF16)

    def row_fn(r, carry):
        r0 = jnp.clip(r - NA_KH // 2, 0, rows - NA_KH)
        d = r - r0
        q = q_ref[0, pl.ds(pl.multiple_of(r * w, w), w), :] * scale
        zero = jnp.zeros_like(q)
        qs = jnp.concatenate([jnp.where(lo, q, zero), jnp.where(lo, zero, q)], axis=0)
        koff = pl.multiple_of(r0 * w, w)
        kk = k_ref[0, pl.ds(koff, nkeys), :]
        vv = v_ref[0, pl.ds(koff, nkeys), :]
        s = lax.dot_general(qs, kk, (((1,), (1,)), ((), ())),
                            preferred_element_type=F32)
        s = s + jnp.concatenate([bias_ref[0, 0, d], bias_ref[0, 1, d]], axis=0)
        m = jnp.max(s, axis=-1, keepdims=True)
        p = jnp.exp(s - m)
        l = jnp.sum(p, axis=-1, keepdims=True)
        o = jnp.dot(p.astype(BF16), vv, preferred_element_type=F32) / l
        out = jnp.where(lo, o[:w], o[w:])
        o_ref[0, pl.ds(pl.multiple_of(r * w, w), w), :] = out.astype(o_ref.dtype)
        return carry

    lax.fori_loop(0, rows, row_fn, 0)


def _na_bias_table(rpb):
    h = rpb.shape[0]
    w = GRID_W
    pad = w - NA_KW
    vp = jnp.pad(rpb.astype(F32), ((0, 0), (0, 0), (pad, pad)))[..., ::-1]
    tt = _toeplitz(vp, w, w)
    c = jnp.arange(w)
    kc = jnp.arange(w)
    c0 = jnp.clip(c - NA_KW // 2, 0, w - NA_KW)
    valid = (kc[None, :] >= c0[:, None]) & (kc[None, :] < c0[:, None] + NA_KW)
    tt = jnp.where(valid, tt, NEG_BIG)
    per_d = [tt[:, NA_KH - 1 - d:2 * NA_KH - 1 - d] for d in range(NA_KH)]
    b = jnp.stack(per_d, axis=1).transpose(0, 1, 3, 2, 4)
    return b.reshape(h // 2, 2, NA_KH, w, NA_KH * w)


def _neighborhood_attention(qkv, bias, batch, seq, d_model):
    hp = NA_HEADS // 2
    rows = seq // GRID_W
    blk = (1, seq, V7X_LANES)
    return pl.pallas_call(
        functools.partial(_na_kernel, rows=rows),
        grid=(hp, batch),
        in_specs=[pl.BlockSpec(blk, lambda h, b: (b, 0, h)),
                  pl.BlockSpec(blk, lambda h, b: (b, 0, hp + h)),
                  pl.BlockSpec(blk, lambda h, b: (b, 0, 2 * hp + h)),
                  pl.BlockSpec((1,) + bias.shape[1:], lambda h, b: (h, 0, 0, 0, 0))],
        out_specs=pl.BlockSpec(blk, lambda h, b: (b, 0, h)),
        out_shape=jax.ShapeDtypeStruct((batch, seq, d_model), BF16),
        compiler_params=_params("parallel", "parallel"),
        name="neighborhood_attention",
    )(qkv, qkv, qkv, bias)


def _t5_bucket(rel):
    half = T5_BUCKETS // 2
    max_exact = half // 2
    sign_off = jnp.where(rel > 0, half, 0)
    n = jnp.abs(rel)
    nf = jnp.maximum(n, 1).astype(F32)
    large = max_exact + (jnp.log(nf / max_exact) / math.log(T5_MAX_DIST / max_exact)
                         * (half - max_exact)).astype(jnp.int32)
    large = jnp.minimum(large, half - 1)
    return sign_off + jnp.where(n < max_exact, n, large)


_DA_NEAR_OFFSETS = (-DA_TK, -DA_TQ, 0, DA_TQ)
_DA_ZERO_TABLE = len(_DA_NEAR_OFFSETS)


def _da_bias_tables(t5_table):
    span = DA_TK + DA_TQ - 1
    rel = jnp.arange(-span, span + 1)
    vals = (t5_table[_t5_bucket(rel)].astype(F32) * LOG2E).T
    vecs = jnp.stack([lax.slice_in_dim(vals, off - (DA_TQ - 1) + span,
                                       off - (DA_TQ - 1) + 2 * span, axis=1)
                      for off in _DA_NEAR_OFFSETS], axis=1)
    near = _toeplitz(vecs, DA_TK, DA_TQ)
    near = jnp.concatenate([near, jnp.zeros_like(near[:, :1])], axis=1)
    far = jnp.stack([vals[:, 0], vals[:, -1]], axis=1)
    return near, far


def _da_kernel(far_ref, q_ref, k_ref, vt_ref, near_ref, lam_ref, g_ref, o_ref,
               qs_ref, st_ref, p_ref, m_ref, l_ref, alpha_ref, acc_ref, *, lambda_init, n_kt):
    h = pl.program_id(0)
    qi = pl.program_id(2)
    tq, tk = DA_TQ, DA_TK
    half = V7X_LANES // 2

    lane = lax.broadcasted_iota(jnp.int32, (tq, V7X_LANES), 1)
    lo = lane < half
    q = q_ref[0]
    zero = jnp.zeros_like(q)
    qs_ref[0:tq, :] = jnp.where(lo, q, zero)
    qs_ref[tq:2 * tq, :] = jnp.where(lo, zero, q)
    m_ref[...] = jnp.full(m_ref.shape, NEG_BIG, F32)
    l_ref[...] = jnp.zeros(l_ref.shape, F32)
    alpha_ref[...] = jnp.zeros(alpha_ref.shape, F32)
    acc_ref[...] = jnp.zeros(acc_ref.shape, F32)

    jn = (qi + 1) // 2 - 1
    par = qi % 2
    c_left = far_ref[h, 0]
    c_right = far_ref[h, 1]

    def tile_of(slot):
        return (jn + slot) & (n_kt - 1)

    def scores(slot):
        off = pl.multiple_of(tile_of(slot) * tk, tk)
        st_ref[slot % 2] = lax.dot_general(
            k_ref[0, pl.ds(off, tk), :], qs_ref[...], (((1,), (1,)), ((), ())),
            preferred_element_type=F32)

    def accumulate(slot):
        off = pl.multiple_of(tile_of(slot) * tk, tk)
        acc_ref[...] = alpha_ref[...] * acc_ref[...] + jnp.dot(
            vt_ref[0, :, pl.ds(off, tk)], p_ref[slot % 2], preferred_element_type=F32)

    def softmax(slot, c, bias):
        st = st_ref[slot % 2]
        if bias is not None:
            st = st + jnp.concatenate([bias, bias], axis=1)
        m_prev = m_ref[...]
        m_next = jnp.maximum(m_prev, jnp.max(st, axis=0, keepdims=True) + c)
        alpha = jnp.exp2(m_prev - m_next)
        p = jnp.exp2(st - (m_next - c))
        l_ref[...] = alpha * l_ref[...] + jnp.sum(p, axis=0, keepdims=True)
        p_ref[slot % 2] = p.astype(BF16)
        alpha_ref[...] = alpha
        m_ref[...] = m_next

    scores(0)
    scores(1)
    banded = jn >= 0
    softmax(0, jnp.where(banded, 0.0, c_right),
            near_ref[0, jnp.where(banded, par, _DA_ZERO_TABLE)])
    scores(2)
    accumulate(0)
    banded = jn + 1 < n_kt
    softmax(1, jnp.where(banded, 0.0, c_left),
            near_ref[0, jnp.where(banded, par + 2, _DA_ZERO_TABLE)])

    def far_c(slot):
        return jnp.where(tile_of(slot) > jn, c_right, c_left)

    for s in range(2, n_kt - 1):
        scores(s + 1)
        accumulate(s - 1)
        softmax(s, far_c(s), None)
    accumulate(n_kt - 2)
    softmax(n_kt - 1, far_c(n_kt - 1), None)
    accumulate(n_kt - 1)

    lam = lam_ref[...]
    lam_full = (jnp.exp(jnp.sum(lam[0:1] * lam[1:2], axis=-1, keepdims=True))
                - jnp.exp(jnp.sum(lam[2:3] * lam[3:4], axis=-1, keepdims=True))
                + lambda_init)
    o_all = acc_ref[...] / l_ref[...]
    o_t = o_all[:, :tq] - lam_full * o_all[:, tq:]
    o = _rms(o_t.T, g_ref[...]) * (1.0 - lambda_init)
    o_ref[0] = o.astype(o_ref.dtype)


def _diff_attention(qk, vt, near, far, lam, subln_g, lambda_init, batch, seq, d_model):
    nh = DA_HEADS
    n_qt = seq // DA_TQ
    n_kt = seq // DA_TK
    assert n_kt & (n_kt - 1) == 0
    q_blk = (1, DA_TQ, V7X_LANES)
    return pl.pallas_call(
        functools.partial(_da_kernel, lambda_init=lambda_init, n_kt=n_kt),
        grid=(nh, batch, n_qt),
        in_specs=[pl.BlockSpec(memory_space=pltpu.SMEM),
                  pl.BlockSpec(q_blk, lambda h, b, i: (b, i, h)),
                  pl.BlockSpec((1, seq, V7X_LANES), lambda h, b, i: (b, 0, nh + h)),
                  pl.BlockSpec((1, V7X_LANES, seq), lambda h, b, i: (b, h, 0)),
                  pl.BlockSpec((1,) + near.shape[1:], lambda h, b, i: (h, 0, 0, 0)),
                  _resident(lam.shape),
                  _resident(subln_g.shape)],
        out_specs=pl.BlockSpec(q_blk, lambda h, b, i: (b, i, h)),
        out_shape=jax.ShapeDtypeStruct((batch, seq, d_model), BF16),
        scratch_shapes=[pltpu.VMEM((2 * DA_TQ, V7X_LANES), BF16),
                        pltpu.VMEM((2, DA_TK, 2 * DA_TQ), F32),
                        pltpu.VMEM((2, DA_TK, 2 * DA_TQ), BF16),
                        pltpu.VMEM((1, 2 * DA_TQ), F32),
                        pltpu.VMEM((1, 2 * DA_TQ), F32),
                        pltpu.VMEM((1, 2 * DA_TQ), F32),
                        pltpu.VMEM((V7X_LANES, 2 * DA_TQ), F32)],
        compiler_params=_params("parallel", "parallel", "arbitrary"),
        name="diff_attention",
    )(far, qk, qk, vt, near, lam, subln_g)


def kernel(x, p, norm_g, na_w_qkv, na_rpb, na_w_o, da_w_qkv, da_lambda, da_subln_g,
           da_w_o, t5_table, ffn_w_in, ffn_conv_w, ffn_conv_b, ffn_w_out,
           ple_w_gate, ple_w_proj):
    batch, seq, d_model = x.shape
    depth = norm_g.shape[0]
    n_tok = batch * seq
    d_ff = ffn_w_out.shape[1]
    x2d = x.reshape(n_tok, d_model)

    for i in range(depth):
        g = norm_g[i].astype(F32)
        j = i // N_MIXERS
        if i % N_MIXERS == 0:
            qkv = _norm_matmul(x2d, g[0:1], na_w_qkv[j].astype(BF16))
            att = _neighborhood_attention(qkv.reshape(batch, seq, 3 * d_model),
                                          _na_bias_table(na_rpb[j]), batch, seq, d_model)
            w_o = na_w_o[j]
        else:
            lambda_init = 0.8 - 0.6 * math.exp(-0.3 * i)
            w = da_w_qkv[j]
            dh = d_model // (2 * DA_HEADS)
            wqk = jnp.concatenate([w[:, :d_model] * (LOG2E * dh ** -0.5),
                                   w[:, d_model:2 * d_model]], axis=1).astype(BF16)
            wvt = w[:, 2 * d_model:].T.astype(BF16)
            qk, vt = _da_proj(x2d, g[0:1], wqk, wvt, batch, seq)
            near, far = _da_bias_tables(t5_table)
            att = _diff_attention(qk.reshape(batch, seq, 2 * d_model), vt, near, far,
                                  da_lambda[j].astype(F32),
                                  da_subln_g[j].astype(F32)[None, :],
                                  lambda_init, batch, seq, d_model)
            w_o = da_w_o[j]
        x2d = _matmul_norm_res(att.reshape(n_tok, d_model), w_o.astype(BF16), g[1:2], x2d)
        w_in = ffn_w_in[i].astype(BF16)
        act = _ffn_in(x2d, g[2:3], w_in[:, :d_ff], w_in[:, d_ff:],
                      ffn_conv_w[i].astype(F32), ffn_conv_b[i].astype(F32)[None, :], seq)
        x2d = _ffn_out_ple(act, ffn_w_out[i].astype(BF16), g[3:4], x2d, g[4:5],
                           ple_w_gate[i].astype(BF16),
                           p[i].reshape(n_tok, p.shape[-1]),
                           ple_w_proj[i].astype(BF16))
    return x2d.reshape(batch, seq, d_model)
```

```python
import functools
import math

import jax
import jax.numpy as jnp
from jax import lax
from jax.experimental import pallas as pl
from jax.experimental.pallas import tpu as pltpu

F32 = jnp.float32
BF16 = jnp.bfloat16

GRID_W = 64
NA_HEADS = 16
NA_KH = 8
NA_KW = 16
DA_HEADS = 8
T5_BUCKETS = 32
T5_MAX_DIST = 128
EPS = 1e-6
N_MIXERS = 2

V7X_LANES = 128
V7X_VMEM_LIMIT_BYTES = 56 * 1024 * 1024

NEG_BIG = -1e30
LOG2E = math.log2(math.e)

ROW_TILE = 512
CONV_HALO = 16
FFN_CHUNK = 256
DA_TQ = 256
DA_TK = 512
DA_SUM_ROWS = 16
NA_ROW_GROUP = 16


def _rms(x, g):
    ms = jnp.mean(x * x, axis=-1, keepdims=True)
    return x * lax.rsqrt(ms + EPS) * g


def _params(*sem):
    return pltpu.CompilerParams(dimension_semantics=sem,
                                vmem_limit_bytes=V7X_VMEM_LIMIT_BYTES)


def _resident(shape):
    nd = len(shape)
    return pl.BlockSpec(shape, lambda *_: (0,) * nd)


def _toeplitz(v, n_rows, n_cols):
    length = n_rows + n_cols - 1
    lead = v.shape[:-1]
    flat = jnp.tile(v, (1,) * len(lead) + (n_rows + 1,))[..., :n_rows * (length + 1)]
    hankel = flat.reshape(lead + (n_rows, length + 1))[..., :n_cols]
    return hankel[..., ::-1]


def _norm_matmul_kernel(x_ref, g_ref, w_ref, o_ref, *, col_chunk):
    h = _rms(x_ref[...], g_ref[...]).astype(BF16)
    for c in range(o_ref.shape[1] // col_chunk):
        sl = slice(c * col_chunk, (c + 1) * col_chunk)
        o_ref[:, sl] = jnp.dot(h, w_ref[:, sl],
                               preferred_element_type=F32).astype(o_ref.dtype)


def _norm_matmul(x2d, g, w):
    n, d = x2d.shape
    nout = w.shape[1]
    return pl.pallas_call(
        functools.partial(_norm_matmul_kernel, col_chunk=1024),
        grid=(n // ROW_TILE,),
        in_specs=[pl.BlockSpec((ROW_TILE, d), lambda i: (i, 0)),
                  _resident((1, d)),
                  _resident((d, nout))],
        out_specs=pl.BlockSpec((ROW_TILE, nout), lambda i: (i, 0)),
        out_shape=jax.ShapeDtypeStruct((n, nout), BF16),
        compiler_params=_params("parallel"),
        name="norm_matmul",
    )(x2d, g, w)


def _da_proj_kernel(x_ref, g_ref, wk_ref, wqvt_ref, k_ref, qvt_ref):
    h = _rms(x_ref[...], g_ref[...]).astype(BF16)
    k_ref[...] = jnp.dot(h, wk_ref[...], preferred_element_type=F32).astype(k_ref.dtype)
    qvt_ref[0] = lax.dot_general(wqvt_ref[...], h, (((1,), (1,)), ((), ())),
                                 preferred_element_type=F32).astype(qvt_ref.dtype)


def _da_proj(x2d, g, wk, wqvt, batch, seq):
    n, d = x2d.shape
    tps = seq // ROW_TILE
    return pl.pallas_call(
        _da_proj_kernel,
        grid=(n // ROW_TILE,),
        in_specs=[pl.BlockSpec((ROW_TILE, d), lambda i: (i, 0)),
                  _resident((1, d)),
                  _resident(wk.shape),
                  _resident(wqvt.shape)],
        out_specs=[pl.BlockSpec((ROW_TILE, wk.shape[1]), lambda i: (i, 0)),
                   pl.BlockSpec((1, wqvt.shape[0], ROW_TILE), lambda i: (i // tps, 0, i % tps))],
        out_shape=[jax.ShapeDtypeStruct((n, wk.shape[1]), BF16),
                   jax.ShapeDtypeStruct((batch, wqvt.shape[0], seq), BF16)],
        compiler_params=_params("parallel"),
        name="da_proj",
    )(x2d, g, wk, wqvt)


def _matmul_norm_res_kernel(a_ref, w_ref, g_ref, x_ref, o_ref):
    f = jnp.dot(a_ref[...], w_ref[...], preferred_element_type=F32)
    o_ref[...] = x_ref[...] + _rms(f, g_ref[...])


def _matmul_norm_res(a2d, w, g, x2d):
    n, d = x2d.shape
    k = a2d.shape[1]
    return pl.pallas_call(
        _matmul_norm_res_kernel,
        grid=(n // ROW_TILE,),
        in_specs=[pl.BlockSpec((ROW_TILE, k), lambda i: (i, 0)),
                  _resident((k, d)),
                  _resident((1, d)),
                  pl.BlockSpec((ROW_TILE, d), lambda i: (i, 0))],
        out_specs=pl.BlockSpec((ROW_TILE, d), lambda i: (i, 0)),
        out_shape=jax.ShapeDtypeStruct((n, d), F32),
        compiler_params=_params("parallel"),
        name="matmul_norm_res",
    )(a2d, w, g, x2d)


def _gelu_tanh(x):
    c = math.sqrt(2.0 / math.pi)
    return 0.5 * x * (1.0 + jnp.tanh(c * (x + 0.044715 * (x * x * x))))


def _ffn_in_kernel(xp_ref, x_ref, xn_ref, g_ref, wg_ref, wv_ref, cw_ref, cb_ref,
                   o_ref, gext_ref, *, tiles_per_seq):
    i = pl.program_id(0)
    t = i % tiles_per_seq
    g = g_ref[...]
    has_prev = (t > 0).astype(F32)
    has_next = (t < tiles_per_seq - 1).astype(F32)
    h_prev = (_rms(xp_ref[...], g) * has_prev).astype(BF16)
    h_main = _rms(x_ref[...], g).astype(BF16)
    h_next = (_rms(xn_ref[...], g) * has_next).astype(BF16)
    h_ext = jnp.concatenate([h_prev, h_main, h_next], axis=0)
    tm = x_ref.shape[0]
    nf = o_ref.shape[1]
    for c in range(nf // FFN_CHUNK):
        sl = slice(c * FFN_CHUNK, (c + 1) * FFN_CHUNK)
        gext_ref[...] = jnp.dot(h_ext, wg_ref[:, sl], preferred_element_type=F32)
        val = jnp.dot(h_main, wv_ref[:, sl], preferred_element_type=F32)
        cw = cw_ref[:, sl]
        gate = (gext_ref[CONV_HALO - 1:CONV_HALO - 1 + tm, :] * cw[0:1]
                + gext_ref[CONV_HALO:CONV_HALO + tm, :] * cw[1:2]
                + gext_ref[CONV_HALO + 1:CONV_HALO + 1 + tm, :] * cw[2:3]
                + cb_ref[:, sl])
        o_ref[:, sl] = (_gelu_tanh(gate) * val).astype(o_ref.dtype)


def _ffn_in(x2d, g, wg, wv, conv_w, conv_b, seq_len):
    n, d = x2d.shape
    nf = wg.shape[1]
    tm = ROW_TILE
    hb = tm // CONV_HALO
    last = n // CONV_HALO - 1
    return pl.pallas_call(
        functools.partial(_ffn_in_kernel, tiles_per_seq=seq_len // tm),
        grid=(n // tm,),
        in_specs=[
            pl.BlockSpec((CONV_HALO, d), lambda i: (jnp.maximum(i * hb - 1, 0), 0)),
            pl.BlockSpec((tm, d), lambda i: (i, 0)),
            pl.BlockSpec((CONV_HALO, d), lambda i: (jnp.minimum((i + 1) * hb, last), 0)),
            _resident((1, d)),
            _resident((d, nf)),
            _resident((d, nf)),
            _resident((3, nf)),
            _resident((1, nf)),
        ],
        out_specs=pl.BlockSpec((tm, nf), lambda i: (i, 0)),
        out_shape=jax.ShapeDtypeStruct((n, nf), BF16),
        scratch_shapes=[pltpu.VMEM((tm + 2 * CONV_HALO, FFN_CHUNK), F32)],
        compiler_params=_params("parallel"),
        name="ffn_in",
    )(x2d, x2d, x2d, g, wg, wv, conv_w, conv_b)


def _ffn_out_ple_kernel(a_ref, wo_ref, g3_ref, x_ref, g4_ref, wgate_ref, p_ref,
                        wproj_ref, o_ref):
    f = jnp.dot(a_ref[...], wo_ref[...], preferred_element_type=F32)
    x2 = x_ref[...] + _rms(f, g3_ref[...])
    hg = _rms(x2, g4_ref[...]).astype(BF16)
    gate = jax.nn.sigmoid(jnp.dot(hg, wgate_ref[...], preferred_element_type=F32))
    emb = jnp.dot(p_ref[...].astype(BF16), wproj_ref[...], preferred_element_type=F32)
    o_ref[...] = x2 + gate * emb


def _ffn_out_ple(a2d, w_out, g3, x2d, g4, w_gate, p2d, w_proj):
    n, d = x2d.shape
    nf = a2d.shape[1]
    pd = p2d.shape[1]
    tm = ROW_TILE
    return pl.pallas_call(
        _ffn_out_ple_kernel,
        grid=(n // tm,),
        in_specs=[pl.BlockSpec((tm, nf), lambda i: (i, 0)),
                  _resident((nf, d)),
                  _resident((1, d)),
                  pl.BlockSpec((tm, d), lambda i: (i, 0)),
                  _resident((1, d)),
                  _resident((d, d)),
                  pl.BlockSpec((tm, pd), lambda i: (i, 0)),
                  _resident((pd, d))],
        out_specs=pl.BlockSpec((tm, d), lambda i: (i, 0)),
        out_shape=jax.ShapeDtypeStruct((n, d), F32),
        compiler_params=_params("parallel"),
        name="ffn_out_ple",
    )(a2d, w_out, g3, x2d, g4, w_gate, p2d, w_proj)


def _na_kernel(q_ref, k_ref, v_ref, bias_ref, o_ref, s_ref, p_ref, *, rows):
    w = GRID_W
    nkeys = NA_KH * w
    group = s_ref.shape[0]
    lane = lax.broadcasted_iota(jnp.int32, (w, V7X_LANES), 1)
    lo = lane < (V7X_LANES // 2)

    def group_fn(gi, carry):
        geo = []
        for i in range(group):
            r = gi * group + i
            r0 = jnp.clip(r - NA_KH // 2, 0, rows - NA_KH)
            geo.append((pl.multiple_of(r * w, w), pl.multiple_of(r0 * w, w), r - r0))
        for i, (qoff, koff, _) in enumerate(geo):
            q = q_ref[0, pl.ds(qoff, w), :]
            zero = jnp.zeros_like(q)
            qs = jnp.concatenate([jnp.where(lo, q, zero), jnp.where(lo, zero, q)], axis=0)
            s_ref[i] = lax.dot_general(qs, k_ref[0, pl.ds(koff, nkeys), :],
                                       (((1,), (1,)), ((), ())),
                                       preferred_element_type=F32)
        sums = []
        for i, (_, _, d) in enumerate(geo):
            s = s_ref[i] + jnp.concatenate([bias_ref[0, 0, d], bias_ref[0, 1, d]], axis=0)
            p = jnp.exp2(s - jnp.max(s, axis=-1, keepdims=True))
            sums.append(jnp.sum(p, axis=-1, keepdims=True))
            p_ref[i] = p.astype(BF16)
        for i, (qoff, koff, _) in enumerate(geo):
            o = jnp.dot(p_ref[i], v_ref[0, pl.ds(koff, nkeys), :],
                        preferred_element_type=F32) / sums[i]
            o_ref[0, pl.ds(qoff, w), :] = jnp.where(lo, o[:w], o[w:]).astype(o_ref.dtype)
        return carry

    lax.fori_loop(0, rows // group, group_fn, 0)


def _na_bias_table(rpb):
    h = rpb.shape[0]
    w = GRID_W
    pad = w - NA_KW
    vp = jnp.pad(rpb.astype(F32), ((0, 0), (0, 0), (pad, pad)))[..., ::-1]
    tt = _toeplitz(vp, w, w)
    c = jnp.arange(w)
    kc = jnp.arange(w)
    c0 = jnp.clip(c - NA_KW // 2, 0, w - NA_KW)
    valid = (kc[None, :] >= c0[:, None]) & (kc[None, :] < c0[:, None] + NA_KW)
    tt = jnp.where(valid, tt * LOG2E, NEG_BIG)
    per_d = [tt[:, NA_KH - 1 - d:2 * NA_KH - 1 - d] for d in range(NA_KH)]
    b = jnp.stack(per_d, axis=1).transpose(0, 1, 3, 2, 4)
    return b.reshape(h // 2, 2, NA_KH, w, NA_KH * w)


def _neighborhood_attention(qkv, bias, batch, seq, d_model):
    hp = NA_HEADS // 2
    rows = seq // GRID_W
    blk = (1, seq, V7X_LANES)
    return pl.pallas_call(
        functools.partial(_na_kernel, rows=rows),
        grid=(hp, batch),
        in_specs=[pl.BlockSpec(blk, lambda h, b: (b, 0, h)),
                  pl.BlockSpec(blk, lambda h, b: (b, 0, hp + h)),
                  pl.BlockSpec(blk, lambda h, b: (b, 0, 2 * hp + h)),
                  pl.BlockSpec((1,) + bias.shape[1:], lambda h, b: (h, 0, 0, 0, 0))],
        out_specs=pl.BlockSpec(blk, lambda h, b: (b, 0, h)),
        out_shape=jax.ShapeDtypeStruct((batch, seq, d_model), BF16),
        scratch_shapes=[pltpu.VMEM((NA_ROW_GROUP, 2 * GRID_W, NA_KH * GRID_W), F32),
                        pltpu.VMEM((NA_ROW_GROUP, 2 * GRID_W, NA_KH * GRID_W), BF16)],
        compiler_params=_params("parallel", "parallel"),
        name="neighborhood_attention",
    )(qkv, qkv, qkv, bias)


def _t5_bucket(rel):
    half = T5_BUCKETS // 2
    max_exact = half // 2
    sign_off = jnp.where(rel > 0, half, 0)
    n = jnp.abs(rel)
    nf = jnp.maximum(n, 1).astype(F32)
    large = max_exact + (jnp.log(nf / max_exact) / math.log(T5_MAX_DIST / max_exact)
                         * (half - max_exact)).astype(jnp.int32)
    large = jnp.minimum(large, half - 1)
    return sign_off + jnp.where(n < max_exact, n, large)


_DA_NEAR_OFFSETS = (-DA_TK, -DA_TQ, 0, DA_TQ)
_DA_ZERO_TABLE = len(_DA_NEAR_OFFSETS)


def _da_bias_tables(t5_table):
    span = DA_TK + DA_TQ - 1
    rel = jnp.arange(-span, span + 1)
    vals = (t5_table[_t5_bucket(rel)].astype(F32) * LOG2E).T
    vecs = jnp.stack([lax.slice_in_dim(vals, off - (DA_TQ - 1) + span,
                                       off - (DA_TQ - 1) + 2 * span, axis=1)
                      for off in _DA_NEAR_OFFSETS], axis=1)
    near = _toeplitz(vecs, DA_TK, DA_TQ)
    near = jnp.concatenate([near, jnp.zeros_like(near[:, :1])], axis=1)
    far = jnp.stack([vals[:, 0], vals[:, -1]], axis=1)
    return near, far


def _da_kernel(far_ref, qt_ref, k_ref, vt_ref, near_ref, lam_ref, g_ref, o_ref,
               qst_ref, st_ref, p_ref, m_ref, alpha_ref, acc_ref, *, lambda_init, n_kt):
    h = pl.program_id(0)
    qi = pl.program_id(2)
    tq, tk = DA_TQ, DA_TK
    nv = V7X_LANES
    half = V7X_LANES // 2

    feat = lax.broadcasted_iota(jnp.int32, (V7X_LANES, tq), 0)
    qt = qt_ref[0]
    zero = jnp.zeros_like(qt)
    qst_ref[:, 0:tq] = jnp.where(feat < half, qt, zero)
    qst_ref[:, tq:2 * tq] = jnp.where(feat < half, zero, qt)
    m_ref[...] = jnp.full(m_ref.shape, NEG_BIG, F32)
    alpha_ref[...] = jnp.zeros(alpha_ref.shape, F32)
    acc_ref[...] = jnp.zeros(acc_ref.shape, F32)
    ones = jnp.ones((DA_SUM_ROWS, tk), BF16)

    jn = (qi + 1) // 2 - 1
    par = qi % 2
    c_left = far_ref[h, 0]
    c_right = far_ref[h, 1]

    def tile_of(slot):
        return (jn + slot) & (n_kt - 1)

    def scores(slot):
        off = pl.multiple_of(tile_of(slot) * tk, tk)
        st_ref[slot % 2] = jnp.dot(k_ref[0, pl.ds(off, tk), :], qst_ref[...],
                                   preferred_element_type=F32)

    def accumulate(slot):
        off = pl.multiple_of(tile_of(slot) * tk, tk)
        lhs = jnp.concatenate([vt_ref[0, :, pl.ds(off, tk)], ones], axis=0)
        acc_ref[...] = alpha_ref[...] * acc_ref[...] + jnp.dot(
            lhs, p_ref[slot % 2], preferred_element_type=F32)

    def softmax(slot, c, bias):
        st = st_ref[slot % 2]
        if bias is not None:
            st = st + jnp.concatenate([bias, bias], axis=1)
        m_prev = m_ref[...]
        m_next = jnp.maximum(m_prev, jnp.max(st, axis=0, keepdims=True) + c)
        p_ref[slot % 2] = jnp.exp2(st - (m_next - c)).astype(BF16)
        alpha_ref[...] = jnp.exp2(m_prev - m_next)
        m_ref[...] = m_next

    scores(0)
    scores(1)
    banded = jn >= 0
    softmax(0, jnp.where(banded, 0.0, c_right),
            near_ref[0, jnp.where(banded, par, _DA_ZERO_TABLE)])
    scores(2)
    accumulate(0)
    banded = jn + 1 < n_kt
    softmax(1, jnp.where(banded, 0.0, c_left),
            near_ref[0, jnp.where(banded, par + 2, _DA_ZERO_TABLE)])

    def far_c(slot):
        return jnp.where(tile_of(slot) > jn, c_right, c_left)

    for s in range(2, n_kt - 1):
        scores(s + 1)
        accumulate(s - 1)
        softmax(s, far_c(s), None)
    accumulate(n_kt - 2)
    softmax(n_kt - 1, far_c(n_kt - 1), None)
    accumulate(n_kt - 1)

    lam = lam_ref[...]
    lam_full = (jnp.exp(jnp.sum(lam[0:1] * lam[1:2], axis=-1, keepdims=True))
                - jnp.exp(jnp.sum(lam[2:3] * lam[3:4], axis=-1, keepdims=True))
                + lambda_init)
    o_all = acc_ref[0:nv, :] / acc_ref[nv:nv + 1, :]
    o_t = o_all[:, :tq] - lam_full * o_all[:, tq:]
    o = _rms(o_t.T, g_ref[...]) * (1.0 - lambda_init)
    o_ref[0] = o.astype(o_ref.dtype)


def _diff_attention(k, qvt, near, far, lam, subln_g, lambda_init, batch, seq, d_model):
    nh = DA_HEADS
    n_qt = seq // DA_TQ
    n_kt = seq // DA_TK
    assert n_kt & (n_kt - 1) == 0
    return pl.pallas_call(
        functools.partial(_da_kernel, lambda_init=lambda_init, n_kt=n_kt),
        grid=(nh, batch, n_qt),
        in_specs=[pl.BlockSpec(memory_space=pltpu.SMEM),
                  pl.BlockSpec((1, V7X_LANES, DA_TQ), lambda h, b, i: (b, h, i)),
                  pl.BlockSpec((1, seq, V7X_LANES), lambda h, b, i: (b, 0, h)),
                  pl.BlockSpec((1, V7X_LANES, seq), lambda h, b, i: (b, nh + h, 0)),
                  pl.BlockSpec((1,) + near.shape[1:], lambda h, b, i: (h, 0, 0, 0)),
                  _resident(lam.shape),
                  _resident(subln_g.shape)],
        out_specs=pl.BlockSpec((1, DA_TQ, V7X_LANES), lambda h, b, i: (b, i, h)),
        out_shape=jax.ShapeDtypeStruct((batch, seq, d_model), BF16),
        scratch_shapes=[pltpu.VMEM((V7X_LANES, 2 * DA_TQ), BF16),
                        pltpu.VMEM((2, DA_TK, 2 * DA_TQ), F32),
                        pltpu.VMEM((2, DA_TK, 2 * DA_TQ), BF16),
                        pltpu.VMEM((1, 2 * DA_TQ), F32),
                        pltpu.VMEM((1, 2 * DA_TQ), F32),
                        pltpu.VMEM((V7X_LANES + DA_SUM_ROWS, 2 * DA_TQ), F32)],
        compiler_params=_params("parallel", "parallel", "arbitrary"),
        name="diff_attention",
    )(far, qvt, k, qvt, near, lam, subln_g)


def kernel(x, p, norm_g, na_w_qkv, na_rpb, na_w_o, da_w_qkv, da_lambda, da_subln_g,
           da_w_o, t5_table, ffn_w_in, ffn_conv_w, ffn_conv_b, ffn_w_out,
           ple_w_gate, ple_w_proj):
    batch, seq, d_model = x.shape
    depth = norm_g.shape[0]
    n_tok = batch * seq
    d_ff = ffn_w_out.shape[1]
    x2d = x.reshape(n_tok, d_model)

    for i in range(depth):
        g = norm_g[i].astype(F32)
        j = i // N_MIXERS
        if i % N_MIXERS == 0:
            w = na_w_qkv[j]
            dh = d_model // NA_HEADS
            w = jnp.concatenate([w[:, :d_model] * (LOG2E * dh ** -0.5), w[:, d_model:]], axis=1)
            qkv = _norm_matmul(x2d, g[0:1], w.astype(BF16))
            att = _neighborhood_attention(qkv.reshape(batch, seq, 3 * d_model),
                                          _na_bias_table(na_rpb[j]), batch, seq, d_model)
            w_o = na_w_o[j]
        else:
            lambda_init = 0.8 - 0.6 * math.exp(-0.3 * i)
            w = da_w_qkv[j]
            dh = d_model // (2 * DA_HEADS)
            wqvt = jnp.concatenate([w[:, :d_model] * (LOG2E * dh ** -0.5),
                                    w[:, 2 * d_model:]], axis=1).T.astype(BF16)
            k, qvt = _da_proj(x2d, g[0:1], w[:, d_model:2 * d_model].astype(BF16), wqvt,
                              batch, seq)
            near, far = _da_bias_tables(t5_table)
            att = _diff_attention(k.reshape(batch, seq, d_model), qvt, near, far,
                                  da_lambda[j].astype(F32),
                                  da_subln_g[j].astype(F32)[None, :],
                                  lambda_init, batch, seq, d_model)
            w_o = da_w_o[j]
        x2d = _matmul_norm_res(att.reshape(n_tok, d_model), w_o.astype(BF16), g[1:2], x2d)
        w_in = ffn_w_in[i].astype(BF16)
        act = _ffn_in(x2d, g[2:3], w_in[:, :d_ff], w_in[:, d_ff:],
                      ffn_conv_w[i].astype(F32), ffn_conv_b[i].astype(F32)[None, :], seq)
        x2d = _ffn_out_ple(act, ffn_w_out[i].astype(BF16), g[3:4], x2d, g[4:5],
                           ple_w_gate[i].astype(BF16),
                           p[i].reshape(n_tok, p.shape[-1]),
                           ple_w_proj[i].astype(BF16))
    return x2d.reshape(batch, seq, d_model)
```

```python
import functools
import math

import jax
import jax.numpy as jnp
from jax import lax
from jax.experimental import pallas as pl
from jax.experimental.pallas import tpu as pltpu

F32 = jnp.float32
BF16 = jnp.bfloat16

GRID_W = 64
NA_HEADS = 16
NA_KH = 8
NA_KW = 16
DA_HEADS = 8
T5_BUCKETS = 32
T5_MAX_DIST = 128
EPS = 1e-6
N_MIXERS = 2

V7X_LANES = 128
V7X_VMEM_LIMIT_BYTES = 56 * 1024 * 1024

NEG_BIG = -1e30
LOG2E = math.log2(math.e)

ROW_TILE = 512
CONV_HALO = 16
FFN_CHUNK = 256
DA_TQ = 256
DA_TK = 512
DA_SUM_ROWS = 16
DA_TILES_PER_TRIP = 4
NA_ROW_GROUP = 16


def _rms(x, g):
    ms = jnp.mean(x * x, axis=-1, keepdims=True)
    return x * lax.rsqrt(ms + EPS) * g


def _params(*sem):
    return pltpu.CompilerParams(dimension_semantics=sem,
                                vmem_limit_bytes=V7X_VMEM_LIMIT_BYTES)


def _resident(shape):
    nd = len(shape)
    return pl.BlockSpec(shape, lambda *_: (0,) * nd)


def _toeplitz(v, n_rows, n_cols):
    length = n_rows + n_cols - 1
    lead = v.shape[:-1]
    flat = jnp.tile(v, (1,) * len(lead) + (n_rows + 1,))[..., :n_rows * (length + 1)]
    hankel = flat.reshape(lead + (n_rows, length + 1))[..., :n_cols]
    return hankel[..., ::-1]


def _norm_matmul_kernel(x_ref, g_ref, w_ref, o_ref, *, col_chunk):
    h = _rms(x_ref[...], g_ref[...]).astype(BF16)
    for c in range(o_ref.shape[1] // col_chunk):
        sl = slice(c * col_chunk, (c + 1) * col_chunk)
        o_ref[:, sl] = jnp.dot(h, w_ref[:, sl],
                               preferred_element_type=F32).astype(o_ref.dtype)


def _norm_matmul(x2d, g, w):
    n, d = x2d.shape
    nout = w.shape[1]
    return pl.pallas_call(
        functools.partial(_norm_matmul_kernel, col_chunk=1024),
        grid=(n // ROW_TILE,),
        in_specs=[pl.BlockSpec((ROW_TILE, d), lambda i: (i, 0)),
                  _resident((1, d)),
                  _resident((d, nout))],
        out_specs=pl.BlockSpec((ROW_TILE, nout), lambda i: (i, 0)),
        out_shape=jax.ShapeDtypeStruct((n, nout), BF16),
        compiler_params=_params("parallel"),
        name="norm_matmul",
    )(x2d, g, w)


def _da_proj_kernel(x_ref, g_ref, wk_ref, wqvt_ref, k_ref, qvt_ref):
    h = _rms(x_ref[...], g_ref[...]).astype(BF16)
    k_ref[...] = jnp.dot(h, wk_ref[...], preferred_element_type=F32).astype(k_ref.dtype)
    qvt_ref[0] = lax.dot_general(wqvt_ref[...], h, (((1,), (1,)), ((), ())),
                                 preferred_element_type=F32).astype(qvt_ref.dtype)


def _da_proj(x2d, g, wk, wqvt, batch, seq):
    n, d = x2d.shape
    tps = seq // ROW_TILE
    return pl.pallas_call(
        _da_proj_kernel,
        grid=(n // ROW_TILE,),
        in_specs=[pl.BlockSpec((ROW_TILE, d), lambda i: (i, 0)),
                  _resident((1, d)),
                  _resident(wk.shape),
                  _resident(wqvt.shape)],
        out_specs=[pl.BlockSpec((ROW_TILE, wk.shape[1]), lambda i: (i, 0)),
                   pl.BlockSpec((1, wqvt.shape[0], ROW_TILE), lambda i: (i // tps, 0, i % tps))],
        out_shape=[jax.ShapeDtypeStruct((n, wk.shape[1]), BF16),
                   jax.ShapeDtypeStruct((batch, wqvt.shape[0], seq), BF16)],
        compiler_params=_params("parallel"),
        name="da_proj",
    )(x2d, g, wk, wqvt)


def _matmul_norm_res_kernel(a_ref, w_ref, g_ref, x_ref, o_ref):
    f = jnp.dot(a_ref[...], w_ref[...], preferred_element_type=F32)
    o_ref[...] = x_ref[...] + _rms(f, g_ref[...])


def _matmul_norm_res(a2d, w, g, x2d):
    n, d = x2d.shape
    k = a2d.shape[1]
    return pl.pallas_call(
        _matmul_norm_res_kernel,
        grid=(n // ROW_TILE,),
        in_specs=[pl.BlockSpec((ROW_TILE, k), lambda i: (i, 0)),
                  _resident((k, d)),
                  _resident((1, d)),
                  pl.BlockSpec((ROW_TILE, d), lambda i: (i, 0))],
        out_specs=pl.BlockSpec((ROW_TILE, d), lambda i: (i, 0)),
        out_shape=jax.ShapeDtypeStruct((n, d), F32),
        compiler_params=_params("parallel"),
        name="matmul_norm_res",
    )(a2d, w, g, x2d)


def _gelu_tanh(x):
    c = math.sqrt(2.0 / math.pi)
    return 0.5 * x * (1.0 + jnp.tanh(c * (x + 0.044715 * (x * x * x))))


def _ffn_in_kernel(xp_ref, x_ref, xn_ref, g_ref, wg_ref, wv_ref, cw_ref, cb_ref,
                   o_ref, *, tiles_per_seq):
    i = pl.program_id(0)
    t = i % tiles_per_seq
    g = g_ref[...]
    has_prev = (t > 0).astype(F32)
    has_next = (t < tiles_per_seq - 1).astype(F32)
    h_prev = (_rms(xp_ref[...], g) * has_prev).astype(BF16)
    h_main = _rms(x_ref[...], g).astype(BF16)
    h_next = (_rms(xn_ref[...], g) * has_next).astype(BF16)
    h_ext = jnp.concatenate([h_prev, h_main, h_next], axis=0)
    tm = x_ref.shape[0]
    nf = o_ref.shape[1]
    for c in range(nf // FFN_CHUNK):
        sl = slice(c * FFN_CHUNK, (c + 1) * FFN_CHUNK)
        gext = jnp.dot(h_ext, wg_ref[:, sl], preferred_element_type=F32)
        val = jnp.dot(h_main, wv_ref[:, sl], preferred_element_type=F32)
        cw = cw_ref[:, sl]
        rows = gext.shape[0]
        g_prev = pltpu.roll(gext, 1, axis=0)[CONV_HALO:CONV_HALO + tm]
        g_next = pltpu.roll(gext, rows - 1, axis=0)[CONV_HALO:CONV_HALO + tm]
        gate = (g_prev * cw[0:1] + gext[CONV_HALO:CONV_HALO + tm] * cw[1:2]
                + g_next * cw[2:3] + cb_ref[:, sl])
        o_ref[:, sl] = (_gelu_tanh(gate) * val).astype(o_ref.dtype)


def _ffn_in(x2d, g, wg, wv, conv_w, conv_b, seq_len):
    n, d = x2d.shape
    nf = wg.shape[1]
    tm = ROW_TILE
    hb = tm // CONV_HALO
    last = n // CONV_HALO - 1
    return pl.pallas_call(
        functools.partial(_ffn_in_kernel, tiles_per_seq=seq_len // tm),
        grid=(n // tm,),
        in_specs=[
            pl.BlockSpec((CONV_HALO, d), lambda i: (jnp.maximum(i * hb - 1, 0), 0)),
            pl.BlockSpec((tm, d), lambda i: (i, 0)),
            pl.BlockSpec((CONV_HALO, d), lambda i: (jnp.minimum((i + 1) * hb, last), 0)),
            _resident((1, d)),
            _resident((d, nf)),
            _resident((d, nf)),
            _resident((3, nf)),
            _resident((1, nf)),
        ],
        out_specs=pl.BlockSpec((tm, nf), lambda i: (i, 0)),
        out_shape=jax.ShapeDtypeStruct((n, nf), BF16),
        compiler_params=_params("parallel"),
        name="ffn_in",
    )(x2d, x2d, x2d, g, wg, wv, conv_w, conv_b)


def _ffn_out_ple_kernel(a_ref, wo_ref, g3_ref, x_ref, g4_ref, wgate_ref, p_ref,
                        wproj_ref, o_ref):
    f = jnp.dot(a_ref[...], wo_ref[...], preferred_element_type=F32)
    x2 = x_ref[...] + _rms(f, g3_ref[...])
    hg = _rms(x2, g4_ref[...]).astype(BF16)
    gate = jax.nn.sigmoid(jnp.dot(hg, wgate_ref[...], preferred_element_type=F32))
    emb = jnp.dot(p_ref[...].astype(BF16), wproj_ref[...], preferred_element_type=F32)
    o_ref[...] = x2 + gate * emb


def _ffn_out_ple(a2d, w_out, g3, x2d, g4, w_gate, p2d, w_proj):
    n, d = x2d.shape
    nf = a2d.shape[1]
    pd = p2d.shape[1]
    tm = ROW_TILE
    return pl.pallas_call(
        _ffn_out_ple_kernel,
        grid=(n // tm,),
        in_specs=[pl.BlockSpec((tm, nf), lambda i: (i, 0)),
                  _resident((nf, d)),
                  _resident((1, d)),
                  pl.BlockSpec((tm, d), lambda i: (i, 0)),
                  _resident((1, d)),
                  _resident((d, d)),
                  pl.BlockSpec((tm, pd), lambda i: (i, 0)),
                  _resident((pd, d))],
        out_specs=pl.BlockSpec((tm, d), lambda i: (i, 0)),
        out_shape=jax.ShapeDtypeStruct((n, d), F32),
        compiler_params=_params("parallel"),
        name="ffn_out_ple",
    )(a2d, w_out, g3, x2d, g4, w_gate, p2d, w_proj)


def _na_kernel(q_ref, k_ref, v_ref, bias_ref, o_ref, s_ref, p_ref, *, rows):
    w = GRID_W
    nkeys = NA_KH * w
    group = s_ref.shape[0]
    lane = lax.broadcasted_iota(jnp.int32, (w, V7X_LANES), 1)
    lo = lane < (V7X_LANES // 2)

    def group_fn(gi, carry):
        geo = []
        for i in range(group):
            r = gi * group + i
            r0 = jnp.clip(r - NA_KH // 2, 0, rows - NA_KH)
            geo.append((pl.multiple_of(r * w, w), pl.multiple_of(r0 * w, w), r - r0))
        for i, (qoff, koff, _) in enumerate(geo):
            q = q_ref[0, pl.ds(qoff, w), :]
            zero = jnp.zeros_like(q)
            qs = jnp.concatenate([jnp.where(lo, q, zero), jnp.where(lo, zero, q)], axis=0)
            s_ref[i] = lax.dot_general(qs, k_ref[0, pl.ds(koff, nkeys), :],
                                       (((1,), (1,)), ((), ())),
                                       preferred_element_type=F32)
        sums = []
        for i, (_, _, d) in enumerate(geo):
            s = s_ref[i] + jnp.concatenate([bias_ref[0, 0, d], bias_ref[0, 1, d]], axis=0)
            p = jnp.exp2(s - jnp.max(s, axis=-1, keepdims=True))
            sums.append(jnp.sum(p, axis=-1, keepdims=True))
            p_ref[i] = p.astype(BF16)
        for i, (qoff, koff, _) in enumerate(geo):
            o = jnp.dot(p_ref[i], v_ref[0, pl.ds(koff, nkeys), :],
                        preferred_element_type=F32) / sums[i]
            o_ref[0, pl.ds(qoff, w), :] = jnp.where(lo, o[:w], o[w:]).astype(o_ref.dtype)
        return carry

    lax.fori_loop(0, rows // group, group_fn, 0)


def _na_bias_table(rpb):
    h = rpb.shape[0]
    w = GRID_W
    pad = w - NA_KW
    vp = jnp.pad(rpb.astype(F32), ((0, 0), (0, 0), (pad, pad)))[..., ::-1]
    tt = _toeplitz(vp, w, w)
    c = jnp.arange(w)
    kc = jnp.arange(w)
    c0 = jnp.clip(c - NA_KW // 2, 0, w - NA_KW)
    valid = (kc[None, :] >= c0[:, None]) & (kc[None, :] < c0[:, None] + NA_KW)
    tt = jnp.where(valid, tt * LOG2E, NEG_BIG)
    per_d = [tt[:, NA_KH - 1 - d:2 * NA_KH - 1 - d] for d in range(NA_KH)]
    b = jnp.stack(per_d, axis=1).transpose(0, 1, 3, 2, 4)
    return b.reshape(h // 2, 2, NA_KH, w, NA_KH * w)


def _neighborhood_attention(qkv, bias, batch, seq, d_model):
    hp = NA_HEADS // 2
    rows = seq // GRID_W
    blk = (1, seq, V7X_LANES)
    return pl.pallas_call(
        functools.partial(_na_kernel, rows=rows),
        grid=(hp, batch),
        in_specs=[pl.BlockSpec(blk, lambda h, b: (b, 0, h)),
                  pl.BlockSpec(blk, lambda h, b: (b, 0, hp + h)),
                  pl.BlockSpec(blk, lambda h, b: (b, 0, 2 * hp + h)),
                  pl.BlockSpec((1,) + bias.shape[1:], lambda h, b: (h, 0, 0, 0, 0))],
        out_specs=pl.BlockSpec(blk, lambda h, b: (b, 0, h)),
        out_shape=jax.ShapeDtypeStruct((batch, seq, d_model), BF16),
        scratch_shapes=[pltpu.VMEM((NA_ROW_GROUP, 2 * GRID_W, NA_KH * GRID_W), F32),
                        pltpu.VMEM((NA_ROW_GROUP, 2 * GRID_W, NA_KH * GRID_W), BF16)],
        compiler_params=_params("parallel", "parallel"),
        name="neighborhood_attention",
    )(qkv, qkv, qkv, bias)


def _t5_bucket(rel):
    half = T5_BUCKETS // 2
    max_exact = half // 2
    sign_off = jnp.where(rel > 0, half, 0)
    n = jnp.abs(rel)
    nf = jnp.maximum(n, 1).astype(F32)
    large = max_exact + (jnp.log(nf / max_exact) / math.log(T5_MAX_DIST / max_exact)
                         * (half - max_exact)).astype(jnp.int32)
    large = jnp.minimum(large, half - 1)
    return sign_off + jnp.where(n < max_exact, n, large)


_DA_NEAR_OFFSETS = (-DA_TK, -DA_TQ, 0, DA_TQ)
_DA_ZERO_TABLE = len(_DA_NEAR_OFFSETS)


def _da_bias_tables(t5_table):
    span = DA_TK + DA_TQ - 1
    rel = jnp.arange(-span, span + 1)
    vals = (t5_table[_t5_bucket(rel)].astype(F32) * LOG2E).T
    vecs = jnp.stack([lax.slice_in_dim(vals, off - (DA_TQ - 1) + span,
                                       off - (DA_TQ - 1) + 2 * span, axis=1)
                      for off in _DA_NEAR_OFFSETS], axis=1)
    near = _toeplitz(vecs, DA_TK, DA_TQ)
    near = jnp.concatenate([near, jnp.zeros_like(near[:, :1])], axis=1)
    far = jnp.stack([vals[:, 0], vals[:, -1]], axis=1)
    return near, far


def _da_kernel(far_ref, qt_ref, k_ref, vt_ref, near_ref, lam_ref, g_ref, o_ref,
               qst_ref, vte_ref, st_ref, p_ref, m_ref, alpha_ref, acc_ref,
               *, lambda_init, n_kt, n_qt):
    h = pl.program_id(0)
    tq, tk = DA_TQ, DA_TK
    nv = V7X_LANES
    half = V7X_LANES // 2
    c_left = far_ref[h, 0]
    c_right = far_ref[h, 1]

    vte_ref[0:nv, :] = vt_ref[0]
    vte_ref[nv:, :] = jnp.ones((DA_SUM_ROWS, vte_ref.shape[1]), BF16)

    lam = lam_ref[...]
    lam_full = (jnp.exp(jnp.sum(lam[0:1] * lam[1:2], axis=-1, keepdims=True))
                - jnp.exp(jnp.sum(lam[2:3] * lam[3:4], axis=-1, keepdims=True))
                + lambda_init)

    def first_tile(qi):
        return (qi + 1) // 2 - 1

    def tile_of(qi, slot):
        return (first_tile(qi) + slot) & (n_kt - 1)

    def scores(qi, slot, bias=None):
        off = pl.multiple_of(tile_of(qi, slot) * tk, tk)
        st = jnp.dot(k_ref[0, pl.ds(off, tk), :], qst_ref[...], preferred_element_type=F32)
        if bias is not None:
            st = st + jnp.concatenate([bias, bias], axis=1)
        st_ref[slot % 2] = st

    def accumulate(qi, slot):
        off = pl.multiple_of(tile_of(qi, slot) * tk, tk)
        acc_ref[...] = alpha_ref[...] * acc_ref[...] + jnp.dot(
            vte_ref[:, pl.ds(off, tk)], p_ref[slot % 2], preferred_element_type=F32)

    def softmax(slot, c):
        st = st_ref[slot % 2]
        m_prev = m_ref[...]
        m_next = jnp.maximum(m_prev, jnp.max(st, axis=0, keepdims=True) + c)
        p_ref[slot % 2] = jnp.exp2(st - (m_next - c)).astype(BF16)
        alpha_ref[...] = jnp.exp2(m_prev - m_next)
        m_ref[...] = m_next

    def banded(qi):
        jn = first_tile(qi)
        return jn >= 0, jn + 1 < n_kt

    def start(qi):
        feat = lax.broadcasted_iota(jnp.int32, (V7X_LANES, tq), 0)
        qt = qt_ref[0, :, pl.ds(pl.multiple_of(qi * tq, tq), tq)]
        zero = jnp.zeros_like(qt)
        qst_ref[:, 0:tq] = jnp.where(feat < half, qt, zero)
        qst_ref[:, tq:2 * tq] = jnp.where(feat < half, zero, qt)
        b0, b1 = banded(qi)
        par = qi % 2
        scores(qi, 0, near_ref[0, jnp.where(b0, par, _DA_ZERO_TABLE)])
        scores(qi, 1, near_ref[0, jnp.where(b1, par + 2, _DA_ZERO_TABLE)])

    def run_slots(qi):
        m_ref[...] = jnp.full(m_ref.shape, NEG_BIG, F32)
        alpha_ref[...] = jnp.zeros(alpha_ref.shape, F32)
        acc_ref[...] = jnp.zeros(acc_ref.shape, F32)
        jn = first_tile(qi)
        b0, b1 = banded(qi)

        def far_c(slot):
            return jnp.where(tile_of(qi, slot) > jn, c_right, c_left)

        softmax(0, jnp.where(b0, 0.0, c_right))
        scores(qi, 2)
        accumulate(qi, 0)
        softmax(1, jnp.where(b1, 0.0, c_left))
        for s in range(2, n_kt - 1):
            scores(qi, s + 1)
            accumulate(qi, s - 1)
            softmax(s, far_c(s))
        accumulate(qi, n_kt - 2)
        softmax(n_kt - 1, far_c(n_kt - 1))
        accumulate(qi, n_kt - 1)

    def finish(qi):
        o_all = acc_ref[0:nv, :] / acc_ref[nv:nv + 1, :]
        o_t = o_all[:, :tq] - lam_full * o_all[:, tq:]
        o = _rms(o_t.T, g_ref[...]) * (1.0 - lambda_init)
        o_ref[0, pl.ds(pl.multiple_of(qi * tq, tq), tq), :] = o.astype(o_ref.dtype)

    def q_group(gi, carry):
        first = gi * DA_TILES_PER_TRIP
        start(first)
        for t in range(DA_TILES_PER_TRIP):
            run_slots(first + t)
            if t + 1 < DA_TILES_PER_TRIP:
                start(first + t + 1)
            finish(first + t)
        return carry

    lax.fori_loop(0, n_qt // DA_TILES_PER_TRIP, q_group, 0)


def _diff_attention(k, qvt, near, far, lam, subln_g, lambda_init, batch, seq, d_model):
    nh = DA_HEADS
    n_qt = seq // DA_TQ
    n_kt = seq // DA_TK
    assert n_kt & (n_kt - 1) == 0
    return pl.pallas_call(
        functools.partial(_da_kernel, lambda_init=lambda_init, n_kt=n_kt, n_qt=n_qt),
        grid=(nh, batch),
        in_specs=[pl.BlockSpec(memory_space=pltpu.SMEM),
                  pl.BlockSpec((1, V7X_LANES, seq), lambda h, b: (b, h, 0)),
                  pl.BlockSpec((1, seq, V7X_LANES), lambda h, b: (b, 0, h)),
                  pl.BlockSpec((1, V7X_LANES, seq), lambda h, b: (b, nh + h, 0)),
                  pl.BlockSpec((1,) + near.shape[1:], lambda h, b: (h, 0, 0, 0)),
                  _resident(lam.shape),
                  _resident(subln_g.shape)],
        out_specs=pl.BlockSpec((1, seq, V7X_LANES), lambda h, b: (b, 0, h)),
        out_shape=jax.ShapeDtypeStruct((batch, seq, d_model), BF16),
        scratch_shapes=[pltpu.VMEM((V7X_LANES, 2 * DA_TQ), BF16),
                        pltpu.VMEM((V7X_LANES + DA_SUM_ROWS, seq), BF16),
                        pltpu.VMEM((2, DA_TK, 2 * DA_TQ), F32),
                        pltpu.VMEM((2, DA_TK, 2 * DA_TQ), BF16),
                        pltpu.VMEM((1, 2 * DA_TQ), F32),
                        pltpu.VMEM((1, 2 * DA_TQ), F32),
                        pltpu.VMEM((V7X_LANES + DA_SUM_ROWS, 2 * DA_TQ), F32)],
        compiler_params=_params("parallel", "parallel"),
        name="diff_attention",
    )(far, qvt, k, qvt, near, lam, subln_g)


def kernel(x, p, norm_g, na_w_qkv, na_rpb, na_w_o, da_w_qkv, da_lambda, da_subln_g,
           da_w_o, t5_table, ffn_w_in, ffn_conv_w, ffn_conv_b, ffn_w_out,
           ple_w_gate, ple_w_proj):
    batch, seq, d_model = x.shape
    depth = norm_g.shape[0]
    n_tok = batch * seq
    d_ff = ffn_w_out.shape[1]
    x2d = x.reshape(n_tok, d_model)

    for i in range(depth):
        g = norm_g[i].astype(F32)
        j = i // N_MIXERS
        if i % N_MIXERS == 0:
            w = na_w_qkv[j]
            dh = d_model // NA_HEADS
            w = jnp.concatenate([w[:, :d_model] * (LOG2E * dh ** -0.5), w[:, d_model:]], axis=1)
            qkv = _norm_matmul(x2d, g[0:1], w.astype(BF16))
            att = _neighborhood_attention(qkv.reshape(batch, seq, 3 * d_model),
                                          _na_bias_table(na_rpb[j]), batch, seq, d_model)
            w_o = na_w_o[j]
        else:
            lambda_init = 0.8 - 0.6 * math.exp(-0.3 * i)
            w = da_w_qkv[j]
            dh = d_model // (2 * DA_HEADS)
            wqvt = jnp.concatenate([w[:, :d_model] * (LOG2E * dh ** -0.5),
                                    w[:, 2 * d_model:]], axis=1).T.astype(BF16)
            k, qvt = _da_proj(x2d, g[0:1], w[:, d_model:2 * d_model].astype(BF16), wqvt,
                              batch, seq)
            near, far = _da_bias_tables(t5_table)
            att = _diff_attention(k.reshape(batch, seq, d_model), qvt, near, far,
                                  da_lambda[j].astype(F32),
                                  da_subln_g[j].astype(F32)[None, :],
                                  lambda_init, batch, seq, d_model)
            w_o = da_w_o[j]
        x2d = _matmul_norm_res(att.reshape(n_tok, d_model), w_o.astype(BF16), g[1:2], x2d)
        w_in = ffn_w_in[i].astype(BF16)
        act = _ffn_in(x2d, g[2:3], w_in[:, :d_ff], w_in[:, d_ff:],
                      ffn_conv_w[i].astype(F32), ffn_conv_b[i].astype(F32)[None, :], seq)
        x2d = _ffn_out_ple(act, ffn_w_out[i].astype(BF16), g[3:4], x2d, g[4:5],
                           ple_w_gate[i].astype(BF16),
                           p[i].reshape(n_tok, p.shape[-1]),
                           ple_w_proj[i].astype(BF16))
    return x2d.reshape(batch, seq, d_model)
```

```python
import functools
import math

import jax
import jax.numpy as jnp
from jax import lax
from jax.experimental import pallas as pl
from jax.experimental.pallas import tpu as pltpu

F32 = jnp.float32
BF16 = jnp.bfloat16

GRID_W = 64
NA_HEADS = 16
NA_KH = 8
NA_KW = 16
DA_HEADS = 8
T5_BUCKETS = 32
T5_MAX_DIST = 128
EPS = 1e-6
N_MIXERS = 2

V7X_LANES = 128
V7X_VMEM_LIMIT_BYTES = 56 * 1024 * 1024

NEG_BIG = -1e30
LOG2E = math.log2(math.e)

ROW_TILE = 512
CONV_HALO = 16
FFN_CHUNK = 256
DA_TQ = 256
DA_TK = 512
DA_SUM_ROWS = 16
DA_TILES_PER_TRIP = 4
NA_ROW_GROUP = 16


def _rms(x, g):
    ms = jnp.mean(x * x, axis=-1, keepdims=True)
    return x * lax.rsqrt(ms + EPS) * g


def _params(*sem):
    return pltpu.CompilerParams(dimension_semantics=sem,
                                vmem_limit_bytes=V7X_VMEM_LIMIT_BYTES)


def _resident(shape):
    nd = len(shape)
    return pl.BlockSpec(shape, lambda *_: (0,) * nd)


def _toeplitz(v, n_rows, n_cols):
    length = n_rows + n_cols - 1
    lead = v.shape[:-1]
    flat = jnp.tile(v, (1,) * len(lead) + (n_rows + 1,))[..., :n_rows * (length + 1)]
    hankel = flat.reshape(lead + (n_rows, length + 1))[..., :n_cols]
    return hankel[..., ::-1]


def _norm_matmul_kernel(x_ref, g_ref, w_ref, o_ref, *, col_chunk):
    h = _rms(x_ref[...], g_ref[...]).astype(BF16)
    for c in range(o_ref.shape[1] // col_chunk):
        sl = slice(c * col_chunk, (c + 1) * col_chunk)
        o_ref[:, sl] = jnp.dot(h, w_ref[:, sl],
                               preferred_element_type=F32).astype(o_ref.dtype)


def _norm_matmul(x2d, g, w):
    n, d = x2d.shape
    nout = w.shape[1]
    return pl.pallas_call(
        functools.partial(_norm_matmul_kernel, col_chunk=1024),
        grid=(n // ROW_TILE,),
        in_specs=[pl.BlockSpec((ROW_TILE, d), lambda i: (i, 0)),
                  _resident((1, d)),
                  _resident((d, nout))],
        out_specs=pl.BlockSpec((ROW_TILE, nout), lambda i: (i, 0)),
        out_shape=jax.ShapeDtypeStruct((n, nout), BF16),
        compiler_params=_params("parallel"),
        name="norm_matmul",
    )(x2d, g, w)


def _da_proj_kernel(x_ref, g_ref, wk_ref, wqvt_ref, k_ref, qvt_ref):
    h = _rms(x_ref[...], g_ref[...]).astype(BF16)
    k_ref[...] = jnp.dot(h, wk_ref[...], preferred_element_type=F32).astype(k_ref.dtype)
    qvt_ref[0] = lax.dot_general(wqvt_ref[...], h, (((1,), (1,)), ((), ())),
                                 preferred_element_type=F32).astype(qvt_ref.dtype)


def _da_proj(x2d, g, wk, wqvt, batch, seq):
    n, d = x2d.shape
    tps = seq // ROW_TILE
    return pl.pallas_call(
        _da_proj_kernel,
        grid=(n // ROW_TILE,),
        in_specs=[pl.BlockSpec((ROW_TILE, d), lambda i: (i, 0)),
                  _resident((1, d)),
                  _resident(wk.shape),
                  _resident(wqvt.shape)],
        out_specs=[pl.BlockSpec((ROW_TILE, wk.shape[1]), lambda i: (i, 0)),
                   pl.BlockSpec((1, wqvt.shape[0], ROW_TILE), lambda i: (i // tps, 0, i % tps))],
        out_shape=[jax.ShapeDtypeStruct((n, wk.shape[1]), BF16),
                   jax.ShapeDtypeStruct((batch, wqvt.shape[0], seq), BF16)],
        compiler_params=_params("parallel"),
        name="da_proj",
    )(x2d, g, wk, wqvt)


def _matmul_norm_res_kernel(a_ref, w_ref, g_ref, x_ref, o_ref):
    f = jnp.dot(a_ref[...], w_ref[...], preferred_element_type=F32)
    o_ref[...] = x_ref[...] + _rms(f, g_ref[...])


def _matmul_norm_res(a2d, w, g, x2d):
    n, d = x2d.shape
    k = a2d.shape[1]
    return pl.pallas_call(
        _matmul_norm_res_kernel,
        grid=(n // ROW_TILE,),
        in_specs=[pl.BlockSpec((ROW_TILE, k), lambda i: (i, 0)),
                  _resident((k, d)),
                  _resident((1, d)),
                  pl.BlockSpec((ROW_TILE, d), lambda i: (i, 0))],
        out_specs=pl.BlockSpec((ROW_TILE, d), lambda i: (i, 0)),
        out_shape=jax.ShapeDtypeStruct((n, d), F32),
        compiler_params=_params("parallel"),
        name="matmul_norm_res",
    )(a2d, w, g, x2d)


def _gelu_tanh(x):
    c = math.sqrt(2.0 / math.pi)
    return 0.5 * x * (1.0 + jnp.tanh(c * (x + 0.044715 * (x * x * x))))


def _ffn_in_kernel(xp_ref, x_ref, xn_ref, g_ref, wg_ref, wv_ref, cw_ref, cb_ref,
                   o_ref, gext_ref, val_ref, *, tiles_per_seq):
    i = pl.program_id(0)
    t = i % tiles_per_seq
    g = g_ref[...]
    has_prev = (t > 0).astype(F32)
    has_next = (t < tiles_per_seq - 1).astype(F32)
    h_prev = (_rms(xp_ref[...], g) * has_prev).astype(BF16)
    h_main = _rms(x_ref[...], g).astype(BF16)
    h_next = (_rms(xn_ref[...], g) * has_next).astype(BF16)
    h_ext = jnp.concatenate([h_prev, h_main, h_next], axis=0)
    tm = x_ref.shape[0]
    n_chunks = o_ref.shape[1] // FFN_CHUNK

    def project(c):
        sl = slice(c * FFN_CHUNK, (c + 1) * FFN_CHUNK)
        gext_ref[c % 2] = jnp.dot(h_ext, wg_ref[:, sl], preferred_element_type=F32)
        val_ref[c % 2] = jnp.dot(h_main, wv_ref[:, sl], preferred_element_type=F32)

    def activate(c):
        sl = slice(c * FFN_CHUNK, (c + 1) * FFN_CHUNK)
        gext = gext_ref[c % 2]
        cw = cw_ref[:, sl]
        rows = gext.shape[0]
        g_prev = pltpu.roll(gext, 1, axis=0)[CONV_HALO:CONV_HALO + tm]
        g_next = pltpu.roll(gext, rows - 1, axis=0)[CONV_HALO:CONV_HALO + tm]
        gate = (g_prev * cw[0:1] + gext[CONV_HALO:CONV_HALO + tm] * cw[1:2]
                + g_next * cw[2:3] + cb_ref[:, sl])
        o_ref[:, sl] = (_gelu_tanh(gate) * val_ref[c % 2]).astype(o_ref.dtype)

    project(0)
    for c in range(n_chunks):
        if c + 1 < n_chunks:
            project(c + 1)
        activate(c)


def _ffn_in(x2d, g, wg, wv, conv_w, conv_b, seq_len):
    n, d = x2d.shape
    nf = wg.shape[1]
    tm = ROW_TILE
    hb = tm // CONV_HALO
    last = n // CONV_HALO - 1
    return pl.pallas_call(
        functools.partial(_ffn_in_kernel, tiles_per_seq=seq_len // tm),
        grid=(n // tm,),
        in_specs=[
            pl.BlockSpec((CONV_HALO, d), lambda i: (jnp.maximum(i * hb - 1, 0), 0)),
            pl.BlockSpec((tm, d), lambda i: (i, 0)),
            pl.BlockSpec((CONV_HALO, d), lambda i: (jnp.minimum((i + 1) * hb, last), 0)),
            _resident((1, d)),
            _resident((d, nf)),
            _resident((d, nf)),
            _resident((3, nf)),
            _resident((1, nf)),
        ],
        out_specs=pl.BlockSpec((tm, nf), lambda i: (i, 0)),
        out_shape=jax.ShapeDtypeStruct((n, nf), BF16),
        scratch_shapes=[pltpu.VMEM((2, tm + 2 * CONV_HALO, FFN_CHUNK), F32),
                        pltpu.VMEM((2, tm, FFN_CHUNK), F32)],
        compiler_params=_params("parallel"),
        name="ffn_in",
    )(x2d, x2d, x2d, g, wg, wv, conv_w, conv_b)


def _ffn_out_ple_kernel(a_ref, wo_ref, g3_ref, x_ref, g4_ref, wgate_ref, p_ref,
                        wproj_ref, o_ref):
    f = jnp.dot(a_ref[...], wo_ref[...], preferred_element_type=F32)
    x2 = x_ref[...] + _rms(f, g3_ref[...])
    hg = _rms(x2, g4_ref[...]).astype(BF16)
    gate = jax.nn.sigmoid(jnp.dot(hg, wgate_ref[...], preferred_element_type=F32))
    emb = jnp.dot(p_ref[...].astype(BF16), wproj_ref[...], preferred_element_type=F32)
    o_ref[...] = x2 + gate * emb


def _ffn_out_ple(a2d, w_out, g3, x2d, g4, w_gate, p2d, w_proj):
    n, d = x2d.shape
    nf = a2d.shape[1]
    pd = p2d.shape[1]
    tm = ROW_TILE
    return pl.pallas_call(
        _ffn_out_ple_kernel,
        grid=(n // tm,),
        in_specs=[pl.BlockSpec((tm, nf), lambda i: (i, 0)),
                  _resident((nf, d)),
                  _resident((1, d)),
                  pl.BlockSpec((tm, d), lambda i: (i, 0)),
                  _resident((1, d)),
                  _resident((d, d)),
                  pl.BlockSpec((tm, pd), lambda i: (i, 0)),
                  _resident((pd, d))],
        out_specs=pl.BlockSpec((tm, d), lambda i: (i, 0)),
        out_shape=jax.ShapeDtypeStruct((n, d), F32),
        compiler_params=_params("parallel"),
        name="ffn_out_ple",
    )(a2d, w_out, g3, x2d, g4, w_gate, p2d, w_proj)


def _na_kernel(q_ref, k_ref, v_ref, bias_ref, o_ref, s_ref, p_ref, *, rows):
    w = GRID_W
    nkeys = NA_KH * w
    group = s_ref.shape[0]
    lane = lax.broadcasted_iota(jnp.int32, (w, V7X_LANES), 1)
    lo = lane < (V7X_LANES // 2)

    def group_fn(gi, carry):
        geo = []
        for i in range(group):
            r = gi * group + i
            r0 = jnp.clip(r - NA_KH // 2, 0, rows - NA_KH)
            geo.append((pl.multiple_of(r * w, w), pl.multiple_of(r0 * w, w), r - r0))
        for i, (qoff, koff, _) in enumerate(geo):
            q = q_ref[0, pl.ds(qoff, w), :]
            zero = jnp.zeros_like(q)
            qs = jnp.concatenate([jnp.where(lo, q, zero), jnp.where(lo, zero, q)], axis=0)
            s_ref[i] = lax.dot_general(qs, k_ref[0, pl.ds(koff, nkeys), :],
                                       (((1,), (1,)), ((), ())),
                                       preferred_element_type=F32)
        sums = []
        for i, (_, _, d) in enumerate(geo):
            first = NA_KH - 1 - d
            bias = [jnp.concatenate([bias_ref[0, hh, first + e] for e in range(0, NA_KH, 2)],
                                    axis=1) for hh in range(2)]
            s = s_ref[i] + jnp.concatenate(bias, axis=0)
            p = jnp.exp2(s - jnp.max(s, axis=-1, keepdims=True))
            sums.append(jnp.sum(p, axis=-1, keepdims=True))
            p_ref[i] = p.astype(BF16)
        for i, (qoff, koff, _) in enumerate(geo):
            o = jnp.dot(p_ref[i], v_ref[0, pl.ds(koff, nkeys), :],
                        preferred_element_type=F32) / sums[i]
            o_ref[0, pl.ds(qoff, w), :] = jnp.where(lo, o[:w], o[w:]).astype(o_ref.dtype)
        return carry

    lax.fori_loop(0, rows // group, group_fn, 0)


def _na_bias_table(rpb):
    h = rpb.shape[0]
    w = GRID_W
    pad = w - NA_KW
    vp = jnp.pad(rpb.astype(F32), ((0, 0), (0, 0), (pad, pad)))[..., ::-1]
    tt = _toeplitz(vp, w, w)
    c = jnp.arange(w)
    kc = jnp.arange(w)
    c0 = jnp.clip(c - NA_KW // 2, 0, w - NA_KW)
    valid = (kc[None, :] >= c0[:, None]) & (kc[None, :] < c0[:, None] + NA_KW)
    tt = jnp.where(valid, tt * LOG2E, NEG_BIG)
    pairs = jnp.concatenate([tt[:, :-1], tt[:, 1:]], axis=-1)
    return pairs.reshape(h // 2, 2, 2 * NA_KH - 2, w, 2 * w)


def _neighborhood_attention(qkv, bias, batch, seq, d_model):
    hp = NA_HEADS // 2
    rows = seq // GRID_W
    blk = (1, seq, V7X_LANES)
    return pl.pallas_call(
        functools.partial(_na_kernel, rows=rows),
        grid=(hp, batch),
        in_specs=[pl.BlockSpec(blk, lambda h, b: (b, 0, h)),
                  pl.BlockSpec(blk, lambda h, b: (b, 0, hp + h)),
                  pl.BlockSpec(blk, lambda h, b: (b, 0, 2 * hp + h)),
                  pl.BlockSpec((1,) + bias.shape[1:], lambda h, b: (h, 0, 0, 0, 0))],
        out_specs=pl.BlockSpec(blk, lambda h, b: (b, 0, h)),
        out_shape=jax.ShapeDtypeStruct((batch, seq, d_model), BF16),
        scratch_shapes=[pltpu.VMEM((NA_ROW_GROUP, 2 * GRID_W, NA_KH * GRID_W), F32),
                        pltpu.VMEM((NA_ROW_GROUP, 2 * GRID_W, NA_KH * GRID_W), BF16)],
        compiler_params=_params("parallel", "parallel"),
        name="neighborhood_attention",
    )(qkv, qkv, qkv, bias)


def _t5_bucket(rel):
    half = T5_BUCKETS // 2
    max_exact = half // 2
    sign_off = jnp.where(rel > 0, half, 0)
    n = jnp.abs(rel)
    nf = jnp.maximum(n, 1).astype(F32)
    large = max_exact + (jnp.log(nf / max_exact) / math.log(T5_MAX_DIST / max_exact)
                         * (half - max_exact)).astype(jnp.int32)
    large = jnp.minimum(large, half - 1)
    return sign_off + jnp.where(n < max_exact, n, large)


DA_BLK = V7X_LANES
DA_BLK_SPAN = 2
assert (DA_BLK_SPAN - 1) * DA_BLK + 1 >= T5_MAX_DIST


def _da_bias_blocks(t5_table):
    span = DA_BLK_SPAN * DA_BLK + DA_BLK - 1
    rel = jnp.arange(-span, span + 1)
    vals = (t5_table[_t5_bucket(rel)].astype(F32) * LOG2E).T
    vecs = jnp.stack([lax.slice_in_dim(vals, DA_BLK * d - (DA_BLK - 1) + span,
                                       DA_BLK * d + DA_BLK + span, axis=1)
                      for d in range(-DA_BLK_SPAN, DA_BLK_SPAN + 1)], axis=1)
    blocks = _toeplitz(vecs, DA_BLK, DA_BLK)
    far = jnp.stack([vals[:, 0], vals[:, -1]], axis=1)
    return blocks, far


def _da_kernel(far_ref, qt_ref, k_ref, vt_ref, blk_ref, lam_ref, g_ref, o_ref,
               qst_ref, vte_ref, st_ref, p_ref, m_ref, alpha_ref, acc_ref,
               *, lambda_init, n_kt, n_qt):
    h = pl.program_id(0)
    tq, tk = DA_TQ, DA_TK
    nv = V7X_LANES
    half = V7X_LANES // 2
    c_left = far_ref[h, 0]
    c_right = far_ref[h, 1]

    vte_ref[0:nv, :] = vt_ref[0]
    vte_ref[nv:, :] = jnp.ones((DA_SUM_ROWS, vte_ref.shape[1]), BF16)

    lam = lam_ref[...]
    lam_full = (jnp.exp(jnp.sum(lam[0:1] * lam[1:2], axis=-1, keepdims=True))
                - jnp.exp(jnp.sum(lam[2:3] * lam[3:4], axis=-1, keepdims=True))
                + lambda_init)

    def first_tile(qi):
        return (qi + 1) // 2 - 1

    def tile_of(qi, slot):
        return (first_tile(qi) + slot) & (n_kt - 1)

    def scores(qi, slot, bias=None):
        off = pl.multiple_of(tile_of(qi, slot) * tk, tk)
        st = jnp.dot(k_ref[0, pl.ds(off, tk), :], qst_ref[...], preferred_element_type=F32)
        if bias is not None:
            st = st + jnp.concatenate([bias, bias], axis=1)
        st_ref[slot % 2] = st

    def accumulate(qi, slot):
        off = pl.multiple_of(tile_of(qi, slot) * tk, tk)
        acc_ref[...] = alpha_ref[...] * acc_ref[...] + jnp.dot(
            vte_ref[:, pl.ds(off, tk)], p_ref[slot % 2], preferred_element_type=F32)

    def softmax(slot, c):
        st = st_ref[slot % 2]
        m_prev = m_ref[...]
        m_next = jnp.maximum(m_prev, jnp.max(st, axis=0, keepdims=True) + c)
        p_ref[slot % 2] = jnp.exp2(st - (m_next - c)).astype(BF16)
        alpha_ref[...] = jnp.exp2(m_prev - m_next)
        m_ref[...] = m_next

    def full_bias(qi, slot):
        j = tile_of(qi, slot)
        rows = []
        for kb in range(tk // DA_BLK):
            d = [(tk // DA_BLK) * j + kb - (tq // DA_BLK) * qi - qb
                 for qb in range(tq // DA_BLK)]
            rows.append(jnp.concatenate(
                [blk_ref[0, jnp.clip(x, -DA_BLK_SPAN, DA_BLK_SPAN) + DA_BLK_SPAN] for x in d],
                axis=1))
        return jnp.concatenate(rows, axis=0)

    def start(qi):
        feat = lax.broadcasted_iota(jnp.int32, (V7X_LANES, tq), 0)
        qt = qt_ref[0, :, pl.ds(pl.multiple_of(qi * tq, tq), tq)]
        zero = jnp.zeros_like(qt)
        qst_ref[:, 0:tq] = jnp.where(feat < half, qt, zero)
        qst_ref[:, tq:2 * tq] = jnp.where(feat < half, zero, qt)
        scores(qi, 0, full_bias(qi, 0))

    def run_slots(qi):
        m_ref[...] = jnp.full(m_ref.shape, NEG_BIG, F32)
        alpha_ref[...] = jnp.zeros(alpha_ref.shape, F32)
        acc_ref[...] = jnp.zeros(acc_ref.shape, F32)
        jn = first_tile(qi)

        def far_c(slot):
            return jnp.where(tile_of(qi, slot) > jn, c_right, c_left)

        scores(qi, 1, full_bias(qi, 1))
        softmax(0, 0.0)
        scores(qi, 2)
        accumulate(qi, 0)
        softmax(1, 0.0)
        for s in range(2, n_kt - 1):
            scores(qi, s + 1)
            accumulate(qi, s - 1)
            softmax(s, far_c(s))
        accumulate(qi, n_kt - 2)
        softmax(n_kt - 1, far_c(n_kt - 1))
        accumulate(qi, n_kt - 1)

    def finish(qi):
        o_all = acc_ref[0:nv, :] / acc_ref[nv:nv + 1, :]
        o_t = o_all[:, :tq] - lam_full * o_all[:, tq:]
        o = _rms(o_t.T, g_ref[...]) * (1.0 - lambda_init)
        o_ref[0, pl.ds(pl.multiple_of(qi * tq, tq), tq), :] = o.astype(o_ref.dtype)

    def q_group(gi, carry):
        first = gi * DA_TILES_PER_TRIP
        start(first)
        for t in range(DA_TILES_PER_TRIP):
            run_slots(first + t)
            if t + 1 < DA_TILES_PER_TRIP:
                start(first + t + 1)
            finish(first + t)
        return carry

    lax.fori_loop(0, n_qt // DA_TILES_PER_TRIP, q_group, 0)


def _diff_attention(k, qvt, blocks, far, lam, subln_g, lambda_init, batch, seq, d_model):
    nh = DA_HEADS
    n_qt = seq // DA_TQ
    n_kt = seq // DA_TK
    assert n_kt & (n_kt - 1) == 0
    return pl.pallas_call(
        functools.partial(_da_kernel, lambda_init=lambda_init, n_kt=n_kt, n_qt=n_qt),
        grid=(nh, batch),
        in_specs=[pl.BlockSpec(memory_space=pltpu.SMEM),
                  pl.BlockSpec((1, V7X_LANES, seq), lambda h, b: (b, h, 0)),
                  pl.BlockSpec((1, seq, V7X_LANES), lambda h, b: (b, 0, h)),
                  pl.BlockSpec((1, V7X_LANES, seq), lambda h, b: (b, nh + h, 0)),
                  pl.BlockSpec((1,) + blocks.shape[1:], lambda h, b: (h, 0, 0, 0)),
                  _resident(lam.shape),
                  _resident(subln_g.shape)],
        out_specs=pl.BlockSpec((1, seq, V7X_LANES), lambda h, b: (b, 0, h)),
        out_shape=jax.ShapeDtypeStruct((batch, seq, d_model), BF16),
        scratch_shapes=[pltpu.VMEM((V7X_LANES, 2 * DA_TQ), BF16),
                        pltpu.VMEM((V7X_LANES + DA_SUM_ROWS, seq), BF16),
                        pltpu.VMEM((2, DA_TK, 2 * DA_TQ), F32),
                        pltpu.VMEM((2, DA_TK, 2 * DA_TQ), BF16),
                        pltpu.VMEM((1, 2 * DA_TQ), F32),
                        pltpu.VMEM((1, 2 * DA_TQ), F32),
                        pltpu.VMEM((V7X_LANES + DA_SUM_ROWS, 2 * DA_TQ), F32)],
        compiler_params=_params("parallel", "parallel"),
        name="diff_attention",
    )(far, qvt, k, qvt, blocks, lam, subln_g)


def kernel(x, p, norm_g, na_w_qkv, na_rpb, na_w_o, da_w_qkv, da_lambda, da_subln_g,
           da_w_o, t5_table, ffn_w_in, ffn_conv_w, ffn_conv_b, ffn_w_out,
           ple_w_gate, ple_w_proj):
    batch, seq, d_model = x.shape
    depth = norm_g.shape[0]
    n_tok = batch * seq
    d_ff = ffn_w_out.shape[1]
    x2d = x.reshape(n_tok, d_model)

    for i in range(depth):
        g = norm_g[i].astype(F32)
        j = i // N_MIXERS
        if i % N_MIXERS == 0:
            w = na_w_qkv[j]
            dh = d_model // NA_HEADS
            w = jnp.concatenate([w[:, :d_model] * (LOG2E * dh ** -0.5), w[:, d_model:]], axis=1)
            qkv = _norm_matmul(x2d, g[0:1], w.astype(BF16))
            att = _neighborhood_attention(qkv.reshape(batch, seq, 3 * d_model),
                                          _na_bias_table(na_rpb[j]), batch, seq, d_model)
            w_o = na_w_o[j]
        else:
            lambda_init = 0.8 - 0.6 * math.exp(-0.3 * i)
            w = da_w_qkv[j]
            dh = d_model // (2 * DA_HEADS)
            wqvt = jnp.concatenate([w[:, :d_model] * (LOG2E * dh ** -0.5),
                                    w[:, 2 * d_model:]], axis=1).T.astype(BF16)
            k, qvt = _da_proj(x2d, g[0:1], w[:, d_model:2 * d_model].astype(BF16), wqvt,
                              batch, seq)
            blocks, far = _da_bias_blocks(t5_table)
            att = _diff_attention(k.reshape(batch, seq, d_model), qvt, blocks, far,
                                  da_lambda[j].astype(F32),
                                  da_subln_g[j].astype(F32)[None, :],
                                  lambda_init, batch, seq, d_model)
            w_o = da_w_o[j]
        x2d = _matmul_norm_res(att.reshape(n_tok, d_model), w_o.astype(BF16), g[1:2], x2d)
        w_in = ffn_w_in[i].astype(BF16)
        act = _ffn_in(x2d, g[2:3], w_in[:, :d_ff], w_in[:, d_ff:],
                      ffn_conv_w[i].astype(F32), ffn_conv_b[i].astype(F32)[None, :], seq)
        x2d = _ffn_out_ple(act, ffn_w_out[i].astype(BF16), g[3:4], x2d, g[4:5],
                           ple_w_gate[i].astype(BF16),
                           p[i].reshape(n_tok, p.shape[-1]),
                           ple_w_proj[i].astype(BF16))
    return x2d.reshape(batch, seq, d_model)
```

```python
import functools
import math

import jax
import jax.numpy as jnp
from jax import lax
from jax.experimental import pallas as pl
from jax.experimental.pallas import tpu as pltpu

F32 = jnp.float32
BF16 = jnp.bfloat16

GRID_W = 64
NA_HEADS = 16
NA_KH = 8
NA_KW = 16
DA_HEADS = 8
T5_BUCKETS = 32
T5_MAX_DIST = 128
EPS = 1e-6
N_MIXERS = 2

V7X_LANES = 128
V7X_VMEM_LIMIT_BYTES = 56 * 1024 * 1024

NEG_BIG = -1e30
LOG2E = math.log2(math.e)

ROW_TILE = 512
FFN_IN_ROW_TILE = 512
CONV_HALO = 16
FFN_CHUNK = 256
DA_TQ = 256
DA_TK = 256
DA_SUM_ROWS = 16
DA_TILES_PER_TRIP = 4
NA_ROW_GROUP = 16


def _rms(x, g):
    ms = jnp.mean(x * x, axis=-1, keepdims=True)
    return x * lax.rsqrt(ms + EPS) * g


def _params(*sem):
    return pltpu.CompilerParams(dimension_semantics=sem,
                                vmem_limit_bytes=V7X_VMEM_LIMIT_BYTES)


def _resident(shape):
    nd = len(shape)
    return pl.BlockSpec(shape, lambda *_: (0,) * nd)


def _toeplitz(v, n_rows, n_cols):
    length = n_rows + n_cols - 1
    lead = v.shape[:-1]
    flat = jnp.tile(v, (1,) * len(lead) + (n_rows + 1,))[..., :n_rows * (length + 1)]
    hankel = flat.reshape(lead + (n_rows, length + 1))[..., :n_cols]
    return hankel[..., ::-1]


def _norm_matmul_kernel(x_ref, g_ref, w_ref, o_ref, *, col_chunk):
    h = _rms(x_ref[...], g_ref[...]).astype(BF16)
    for c in range(o_ref.shape[1] // col_chunk):
        sl = slice(c * col_chunk, (c + 1) * col_chunk)
        o_ref[:, sl] = jnp.dot(h, w_ref[:, sl],
                               preferred_element_type=F32).astype(o_ref.dtype)


def _norm_matmul(x2d, g, w):
    n, d = x2d.shape
    nout = w.shape[1]
    return pl.pallas_call(
        functools.partial(_norm_matmul_kernel, col_chunk=1024),
        grid=(n // ROW_TILE,),
        in_specs=[pl.BlockSpec((ROW_TILE, d), lambda i: (i, 0)),
                  _resident((1, d)),
                  _resident((d, nout))],
        out_specs=pl.BlockSpec((ROW_TILE, nout), lambda i: (i, 0)),
        out_shape=jax.ShapeDtypeStruct((n, nout), BF16),
        compiler_params=_params("parallel"),
        name="norm_matmul",
    )(x2d, g, w)


def _da_proj_kernel(x_ref, g_ref, wk_ref, wqvt_ref, k_ref, qvt_ref):
    h = _rms(x_ref[...], g_ref[...]).astype(BF16)
    k_ref[...] = jnp.dot(h, wk_ref[...], preferred_element_type=F32).astype(k_ref.dtype)
    qvt_ref[0] = lax.dot_general(wqvt_ref[...], h, (((1,), (1,)), ((), ())),
                                 preferred_element_type=F32).astype(qvt_ref.dtype)


def _da_proj(x2d, g, wk, wqvt, batch, seq):
    n, d = x2d.shape
    tps = seq // ROW_TILE
    return pl.pallas_call(
        _da_proj_kernel,
        grid=(n // ROW_TILE,),
        in_specs=[pl.BlockSpec((ROW_TILE, d), lambda i: (i, 0)),
                  _resident((1, d)),
                  _resident(wk.shape),
                  _resident(wqvt.shape)],
        out_specs=[pl.BlockSpec((ROW_TILE, wk.shape[1]), lambda i: (i, 0)),
                   pl.BlockSpec((1, wqvt.shape[0], ROW_TILE), lambda i: (i // tps, 0, i % tps))],
        out_shape=[jax.ShapeDtypeStruct((n, wk.shape[1]), BF16),
                   jax.ShapeDtypeStruct((batch, wqvt.shape[0], seq), BF16)],
        compiler_params=_params("parallel"),
        name="da_proj",
    )(x2d, g, wk, wqvt)


def _matmul_norm_res_kernel(a_ref, w_ref, g_ref, x_ref, o_ref):
    f = jnp.dot(a_ref[...], w_ref[...], preferred_element_type=F32)
    o_ref[...] = x_ref[...] + _rms(f, g_ref[...])


def _matmul_norm_res(a2d, w, g, x2d):
    n, d = x2d.shape
    k = a2d.shape[1]
    return pl.pallas_call(
        _matmul_norm_res_kernel,
        grid=(n // ROW_TILE,),
        in_specs=[pl.BlockSpec((ROW_TILE, k), lambda i: (i, 0)),
                  _resident((k, d)),
                  _resident((1, d)),
                  pl.BlockSpec((ROW_TILE, d), lambda i: (i, 0))],
        out_specs=pl.BlockSpec((ROW_TILE, d), lambda i: (i, 0)),
        out_shape=jax.ShapeDtypeStruct((n, d), F32),
        compiler_params=_params("parallel"),
        name="matmul_norm_res",
    )(a2d, w, g, x2d)


def _gelu_tanh(x):
    c = math.sqrt(2.0 / math.pi)
    return 0.5 * x * (1.0 + jnp.tanh(c * (x + 0.044715 * (x * x * x))))


def _ffn_in_kernel(xp_ref, x_ref, xn_ref, g_ref, wg_ref, wv_ref, cw_ref, cb_ref,
                   o_ref, gext_ref, val_ref, *, tiles_per_seq):
    i = pl.program_id(0)
    t = i % tiles_per_seq
    g = g_ref[...]
    has_prev = (t > 0).astype(F32)
    has_next = (t < tiles_per_seq - 1).astype(F32)
    h_prev = (_rms(xp_ref[...], g) * has_prev).astype(BF16)
    h_main = _rms(x_ref[...], g).astype(BF16)
    h_next = (_rms(xn_ref[...], g) * has_next).astype(BF16)
    h_ext = jnp.concatenate([h_prev, h_main, h_next], axis=0)
    tm = x_ref.shape[0]
    n_chunks = o_ref.shape[1] // FFN_CHUNK

    def project(c):
        sl = slice(c * FFN_CHUNK, (c + 1) * FFN_CHUNK)
        gext_ref[c % 2] = jnp.dot(h_ext, wg_ref[:, sl], preferred_element_type=F32)
        val_ref[c % 2] = jnp.dot(h_main, wv_ref[:, sl], preferred_element_type=F32)

    def activate(c):
        sl = slice(c * FFN_CHUNK, (c + 1) * FFN_CHUNK)
        gext = gext_ref[c % 2]
        cw = cw_ref[:, sl]
        rows = gext.shape[0]
        g_prev = pltpu.roll(gext, 1, axis=0)[CONV_HALO:CONV_HALO + tm]
        g_next = pltpu.roll(gext, rows - 1, axis=0)[CONV_HALO:CONV_HALO + tm]
        gate = (g_prev * cw[0:1] + gext[CONV_HALO:CONV_HALO + tm] * cw[1:2]
                + g_next * cw[2:3] + cb_ref[:, sl])
        o_ref[:, sl] = (_gelu_tanh(gate) * val_ref[c % 2]).astype(o_ref.dtype)

    project(0)
    for c in range(n_chunks):
        if c + 1 < n_chunks:
            project(c + 1)
        activate(c)


def _ffn_in(x2d, g, wg, wv, conv_w, conv_b, seq_len):
    n, d = x2d.shape
    nf = wg.shape[1]
    tm = FFN_IN_ROW_TILE
    hb = tm // CONV_HALO
    last = n // CONV_HALO - 1
    return pl.pallas_call(
        functools.partial(_ffn_in_kernel, tiles_per_seq=seq_len // tm),
        grid=(n // tm,),
        in_specs=[
            pl.BlockSpec((CONV_HALO, d), lambda i: (jnp.maximum(i * hb - 1, 0), 0)),
            pl.BlockSpec((tm, d), lambda i: (i, 0)),
            pl.BlockSpec((CONV_HALO, d), lambda i: (jnp.minimum((i + 1) * hb, last), 0)),
            _resident((1, d)),
            _resident((d, nf)),
            _resident((d, nf)),
            _resident((3, nf)),
            _resident((1, nf)),
        ],
        out_specs=pl.BlockSpec((tm, nf), lambda i: (i, 0)),
        out_shape=jax.ShapeDtypeStruct((n, nf), BF16),
        scratch_shapes=[pltpu.VMEM((2, tm + 2 * CONV_HALO, FFN_CHUNK), F32),
                        pltpu.VMEM((2, tm, FFN_CHUNK), F32)],
        compiler_params=_params("parallel"),
        name="ffn_in",
    )(x2d, x2d, x2d, g, wg, wv, conv_w, conv_b)


def _ffn_out_ple_kernel(a_ref, wo_ref, g3_ref, x_ref, g4_ref, wgate_ref, p_ref,
                        wproj_ref, o_ref):
    f = jnp.dot(a_ref[...], wo_ref[...], preferred_element_type=F32)
    x2 = x_ref[...] + _rms(f, g3_ref[...])
    hg = _rms(x2, g4_ref[...]).astype(BF16)
    gate = jax.nn.sigmoid(jnp.dot(hg, wgate_ref[...], preferred_element_type=F32))
    emb = jnp.dot(p_ref[...].astype(BF16), wproj_ref[...], preferred_element_type=F32)
    o_ref[...] = x2 + gate * emb


def _ffn_out_ple(a2d, w_out, g3, x2d, g4, w_gate, p2d, w_proj):
    n, d = x2d.shape
    nf = a2d.shape[1]
    pd = p2d.shape[1]
    tm = ROW_TILE
    return pl.pallas_call(
        _ffn_out_ple_kernel,
        grid=(n // tm,),
        in_specs=[pl.BlockSpec((tm, nf), lambda i: (i, 0)),
                  _resident((nf, d)),
                  _resident((1, d)),
                  pl.BlockSpec((tm, d), lambda i: (i, 0)),
                  _resident((1, d)),
                  _resident((d, d)),
                  pl.BlockSpec((tm, pd), lambda i: (i, 0)),
                  _resident((pd, d))],
        out_specs=pl.BlockSpec((tm, d), lambda i: (i, 0)),
        out_shape=jax.ShapeDtypeStruct((n, d), F32),
        compiler_params=_params("parallel"),
        name="ffn_out_ple",
    )(a2d, w_out, g3, x2d, g4, w_gate, p2d, w_proj)


def _na_kernel(q_ref, k_ref, v_ref, bias_ref, o_ref, s_ref, p_ref, *, rows):
    w = GRID_W
    nkeys = NA_KH * w
    group = s_ref.shape[0]
    lane = lax.broadcasted_iota(jnp.int32, (w, V7X_LANES), 1)
    lo = lane < (V7X_LANES // 2)

    def group_fn(gi, carry):
        geo = []
        for i in range(group):
            r = gi * group + i
            r0 = jnp.clip(r - NA_KH // 2, 0, rows - NA_KH)
            geo.append((pl.multiple_of(r * w, w), pl.multiple_of(r0 * w, w), r - r0))
        for i, (qoff, koff, _) in enumerate(geo):
            q = q_ref[0, pl.ds(qoff, w), :]
            zero = jnp.zeros_like(q)
            qs = jnp.concatenate([jnp.where(lo, q, zero), jnp.where(lo, zero, q)], axis=0)
            s_ref[i] = lax.dot_general(qs, k_ref[0, pl.ds(koff, nkeys), :],
                                       (((1,), (1,)), ((), ())),
                                       preferred_element_type=F32)
        sums = []
        for i, (_, _, d) in enumerate(geo):
            first = NA_KH - 1 - d
            bias = [jnp.concatenate([bias_ref[0, hh, first + e] for e in range(0, NA_KH, 2)],
                                    axis=1) for hh in range(2)]
            s = s_ref[i] + jnp.concatenate(bias, axis=0)
            p = jnp.exp2(s - jnp.max(s, axis=-1, keepdims=True))
            sums.append(jnp.sum(p, axis=-1, keepdims=True))
            p_ref[i] = p.astype(BF16)
        for i, (qoff, koff, _) in enumerate(geo):
            o = jnp.dot(p_ref[i], v_ref[0, pl.ds(koff, nkeys), :],
                        preferred_element_type=F32) / sums[i]
            o_ref[0, pl.ds(qoff, w), :] = jnp.where(lo, o[:w], o[w:]).astype(o_ref.dtype)
        return carry

    lax.fori_loop(0, rows // group, group_fn, 0)


def _na_bias_table(rpb):
    h = rpb.shape[0]
    w = GRID_W
    pad = w - NA_KW
    vp = jnp.pad(rpb.astype(F32), ((0, 0), (0, 0), (pad, pad)))[..., ::-1]
    tt = _toeplitz(vp, w, w)
    c = jnp.arange(w)
    kc = jnp.arange(w)
    c0 = jnp.clip(c - NA_KW // 2, 0, w - NA_KW)
    valid = (kc[None, :] >= c0[:, None]) & (kc[None, :] < c0[:, None] + NA_KW)
    tt = jnp.where(valid, tt * LOG2E, NEG_BIG)
    pairs = jnp.concatenate([tt[:, :-1], tt[:, 1:]], axis=-1)
    return pairs.reshape(h // 2, 2, 2 * NA_KH - 2, w, 2 * w)


def _neighborhood_attention(qkv, bias, batch, seq, d_model):
    hp = NA_HEADS // 2
    rows = seq // GRID_W
    blk = (1, seq, V7X_LANES)
    return pl.pallas_call(
        functools.partial(_na_kernel, rows=rows),
        grid=(hp, batch),
        in_specs=[pl.BlockSpec(blk, lambda h, b: (b, 0, h)),
                  pl.BlockSpec(blk, lambda h, b: (b, 0, hp + h)),
                  pl.BlockSpec(blk, lambda h, b: (b, 0, 2 * hp + h)),
                  pl.BlockSpec((1,) + bias.shape[1:], lambda h, b: (h, 0, 0, 0, 0))],
        out_specs=pl.BlockSpec(blk, lambda h, b: (b, 0, h)),
        out_shape=jax.ShapeDtypeStruct((batch, seq, d_model), BF16),
        scratch_shapes=[pltpu.VMEM((NA_ROW_GROUP, 2 * GRID_W, NA_KH * GRID_W), F32),
                        pltpu.VMEM((NA_ROW_GROUP, 2 * GRID_W, NA_KH * GRID_W), BF16)],
        compiler_params=_params("parallel", "parallel"),
        name="neighborhood_attention",
    )(qkv, qkv, qkv, bias)


def _t5_bucket(rel):
    half = T5_BUCKETS // 2
    max_exact = half // 2
    sign_off = jnp.where(rel > 0, half, 0)
    n = jnp.abs(rel)
    nf = jnp.maximum(n, 1).astype(F32)
    large = max_exact + (jnp.log(nf / max_exact) / math.log(T5_MAX_DIST / max_exact)
                         * (half - max_exact)).astype(jnp.int32)
    large = jnp.minimum(large, half - 1)
    return sign_off + jnp.where(n < max_exact, n, large)


DA_BLK = V7X_LANES
DA_BLK_SPAN = 2
assert (DA_BLK_SPAN - 1) * DA_BLK + 1 >= T5_MAX_DIST


def _da_bias_blocks(t5_table):
    span = DA_BLK_SPAN * DA_BLK + DA_BLK - 1
    rel = jnp.arange(-span, span + 1)
    vals = (t5_table[_t5_bucket(rel)].astype(F32) * LOG2E).T
    vecs = jnp.stack([lax.slice_in_dim(vals, DA_BLK * d - (DA_BLK - 1) + span,
                                       DA_BLK * d + DA_BLK + span, axis=1)
                      for d in range(-DA_BLK_SPAN, DA_BLK_SPAN + 1)], axis=1)
    blocks = _toeplitz(vecs, DA_BLK, DA_BLK)
    far = jnp.stack([vals[:, 0], vals[:, -1]], axis=1)
    return blocks, far


def _da_kernel(far_ref, qt_ref, k_ref, vt_ref, blk_ref, lam_ref, g_ref, o_ref,
               qst_ref, vte_ref, st_ref, p_ref, m_ref, alpha_ref, acc_ref,
               *, lambda_init, n_kt, n_qt):
    h = pl.program_id(0)
    tq, tk = DA_TQ, DA_TK
    nv = V7X_LANES
    half = V7X_LANES // 2
    c_left = far_ref[h, 0]
    c_right = far_ref[h, 1]

    vte_ref[0:nv, :] = vt_ref[0]
    vte_ref[nv:, :] = jnp.ones((DA_SUM_ROWS, vte_ref.shape[1]), BF16)

    lam = lam_ref[...]
    lam_full = (jnp.exp(jnp.sum(lam[0:1] * lam[1:2], axis=-1, keepdims=True))
                - jnp.exp(jnp.sum(lam[2:3] * lam[3:4], axis=-1, keepdims=True))
                + lambda_init)

    def first_tile(qi):
        return (qi * tq + tk - (T5_MAX_DIST - 1)) // tk - 1

    n_band = max((q0 + tq - 1 + T5_MAX_DIST - 1) // tk - (q0 - (T5_MAX_DIST - 1)) // tk + 1
                 for q0 in range(0, n_qt * tq, tq))

    def tile_of(qi, slot):
        return (first_tile(qi) + slot) & (n_kt - 1)

    def scores(qi, slot, bias=None):
        off = pl.multiple_of(tile_of(qi, slot) * tk, tk)
        st = jnp.dot(k_ref[0, pl.ds(off, tk), :], qst_ref[...], preferred_element_type=F32)
        if bias is not None:
            st = st + jnp.concatenate([bias, bias], axis=1)
        st_ref[slot % 2] = st

    def accumulate(qi, slot):
        off = pl.multiple_of(tile_of(qi, slot) * tk, tk)
        acc_ref[...] = alpha_ref[...] * acc_ref[...] + jnp.dot(
            vte_ref[:, pl.ds(off, tk)], p_ref[slot % 2], preferred_element_type=F32)

    def softmax(slot, c):
        st = st_ref[slot % 2]
        m_prev = m_ref[...]
        m_next = jnp.maximum(m_prev, jnp.max(st, axis=0, keepdims=True) + c)
        p_ref[slot % 2] = jnp.exp2(st - (m_next - c)).astype(BF16)
        alpha_ref[...] = jnp.exp2(m_prev - m_next)
        m_ref[...] = m_next

    def full_bias(qi, slot):
        j = tile_of(qi, slot)
        rows = []
        for kb in range(tk // DA_BLK):
            d = [(tk // DA_BLK) * j + kb - (tq // DA_BLK) * qi - qb
                 for qb in range(tq // DA_BLK)]
            rows.append(jnp.concatenate(
                [blk_ref[0, jnp.clip(x, -DA_BLK_SPAN, DA_BLK_SPAN) + DA_BLK_SPAN] for x in d],
                axis=1))
        return jnp.concatenate(rows, axis=0)

    def start(qi):
        feat = lax.broadcasted_iota(jnp.int32, (V7X_LANES, tq), 0)
        qt = qt_ref[0, :, pl.ds(pl.multiple_of(qi * tq, tq), tq)]
        zero = jnp.zeros_like(qt)
        qst_ref[:, 0:tq] = jnp.where(feat < half, qt, zero)
        qst_ref[:, tq:2 * tq] = jnp.where(feat < half, zero, qt)
        scores(qi, 0, full_bias(qi, 0))

    def run_slots(qi):
        m_ref[...] = jnp.full(m_ref.shape, NEG_BIG, F32)
        alpha_ref[...] = jnp.zeros(alpha_ref.shape, F32)
        acc_ref[...] = jnp.zeros(acc_ref.shape, F32)
        jn = first_tile(qi)

        def far_c(slot):
            return jnp.where(tile_of(qi, slot) > jn, c_right, c_left)

        def bias_of(slot):
            return full_bias(qi, slot) if slot < n_band else None

        scores(qi, 1, bias_of(1))
        softmax(0, 0.0)
        for s in range(1, n_kt - 1):
            scores(qi, s + 1, bias_of(s + 1))
            accumulate(qi, s - 1)
            softmax(s, 0.0 if s < n_band else far_c(s))
        accumulate(qi, n_kt - 2)
        softmax(n_kt - 1, far_c(n_kt - 1))
        accumulate(qi, n_kt - 1)

    def finish(qi):
        o_all = acc_ref[0:nv, :] / acc_ref[nv:nv + 1, :]
        o_t = o_all[:, :tq] - lam_full * o_all[:, tq:]
        o = _rms(o_t.T, g_ref[...]) * (1.0 - lambda_init)
        o_ref[0, pl.ds(pl.multiple_of(qi * tq, tq), tq), :] = o.astype(o_ref.dtype)

    def q_group(gi, carry):
        first = gi * DA_TILES_PER_TRIP
        start(first)
        for t in range(DA_TILES_PER_TRIP):
            run_slots(first + t)
            if t + 1 < DA_TILES_PER_TRIP:
                start(first + t + 1)
            finish(first + t)
        return carry

    lax.fori_loop(0, n_qt // DA_TILES_PER_TRIP, q_group, 0)


def _diff_attention(k, qvt, blocks, far, lam, subln_g, lambda_init, batch, seq, d_model):
    nh = DA_HEADS
    n_qt = seq // DA_TQ
    n_kt = seq // DA_TK
    assert n_kt & (n_kt - 1) == 0
    return pl.pallas_call(
        functools.partial(_da_kernel, lambda_init=lambda_init, n_kt=n_kt, n_qt=n_qt),
        grid=(nh, batch),
        in_specs=[pl.BlockSpec(memory_space=pltpu.SMEM),
                  pl.BlockSpec((1, V7X_LANES, seq), lambda h, b: (b, h, 0)),
                  pl.BlockSpec((1, seq, V7X_LANES), lambda h, b: (b, 0, h)),
                  pl.BlockSpec((1, V7X_LANES, seq), lambda h, b: (b, nh + h, 0)),
                  pl.BlockSpec((1,) + blocks.shape[1:], lambda h, b: (h, 0, 0, 0)),
                  _resident(lam.shape),
                  _resident(subln_g.shape)],
        out_specs=pl.BlockSpec((1, seq, V7X_LANES), lambda h, b: (b, 0, h)),
        out_shape=jax.ShapeDtypeStruct((batch, seq, d_model), BF16),
        scratch_shapes=[pltpu.VMEM((V7X_LANES, 2 * DA_TQ), BF16),
                        pltpu.VMEM((V7X_LANES + DA_SUM_ROWS, seq), BF16),
                        pltpu.VMEM((2, DA_TK, 2 * DA_TQ), F32),
                        pltpu.VMEM((2, DA_TK, 2 * DA_TQ), BF16),
                        pltpu.VMEM((1, 2 * DA_TQ), F32),
                        pltpu.VMEM((1, 2 * DA_TQ), F32),
                        pltpu.VMEM((V7X_LANES + DA_SUM_ROWS, 2 * DA_TQ), F32)],
        compiler_params=_params("parallel", "parallel"),
        name="diff_attention",
    )(far, qvt, k, qvt, blocks, lam, subln_g)


def kernel(x, p, norm_g, na_w_qkv, na_rpb, na_w_o, da_w_qkv, da_lambda, da_subln_g,
           da_w_o, t5_table, ffn_w_in, ffn_conv_w, ffn_conv_b, ffn_w_out,
           ple_w_gate, ple_w_proj):
    batch, seq, d_model = x.shape
    depth = norm_g.shape[0]
    n_tok = batch * seq
    d_ff = ffn_w_out.shape[1]
    x2d = x.reshape(n_tok, d_model)

    for i in range(depth):
        g = norm_g[i].astype(F32)
        j = i // N_MIXERS
        if i % N_MIXERS == 0:
            w = na_w_qkv[j]
            dh = d_model // NA_HEADS
            w = jnp.concatenate([w[:, :d_model] * (LOG2E * dh ** -0.5), w[:, d_model:]], axis=1)
            qkv = _norm_matmul(x2d, g[0:1], w.astype(BF16))
            att = _neighborhood_attention(qkv.reshape(batch, seq, 3 * d_model),
                                          _na_bias_table(na_rpb[j]), batch, seq, d_model)
            w_o = na_w_o[j]
        else:
            lambda_init = 0.8 - 0.6 * math.exp(-0.3 * i)
            w = da_w_qkv[j]
            dh = d_model // (2 * DA_HEADS)
            wqvt = jnp.concatenate([w[:, :d_model] * (LOG2E * dh ** -0.5),
                                    w[:, 2 * d_model:]], axis=1).T.astype(BF16)
            k, qvt = _da_proj(x2d, g[0:1], w[:, d_model:2 * d_model].astype(BF16), wqvt,
                              batch, seq)
            blocks, far = _da_bias_blocks(t5_table)
            att = _diff_attention(k.reshape(batch, seq, d_model), qvt, blocks, far,
                                  da_lambda[j].astype(F32),
                                  da_subln_g[j].astype(F32)[None, :],
                                  lambda_init, batch, seq, d_model)
            w_o = da_w_o[j]
        x2d = _matmul_norm_res(att.reshape(n_tok, d_model), w_o.astype(BF16), g[1:2], x2d)
        w_in = ffn_w_in[i].astype(BF16)
        act = _ffn_in(x2d, g[2:3], w_in[:, :d_ff], w_in[:, d_ff:],
                      ffn_conv_w[i].astype(F32), ffn_conv_b[i].astype(F32)[None, :], seq)
        x2d = _ffn_out_ple(act, ffn_w_out[i].astype(BF16), g[3:4], x2d, g[4:5],
                           ple_w_gate[i].astype(BF16),
                           p[i].reshape(n_tok, p.shape[-1]),
                           ple_w_proj[i].astype(BF16))
    return x2d.reshape(batch, seq, d_model)
```

```python
import functools
import math

import jax
import jax.numpy as jnp
from jax import lax
from jax.experimental import pallas as pl
from jax.experimental.pallas import tpu as pltpu

F32 = jnp.float32
BF16 = jnp.bfloat16

GRID_W = 64
NA_HEADS = 16
NA_KH = 8
NA_KW = 16
DA_HEADS = 8
T5_BUCKETS = 32
T5_MAX_DIST = 128
EPS = 1e-6
N_MIXERS = 2

V7X_LANES = 128
V7X_VMEM_LIMIT_BYTES = 56 * 1024 * 1024

NEG_BIG = -1e30
LOG2E = math.log2(math.e)

ROW_TILE = 512
FFN_IN_ROW_TILE = 512
OUT_PROJ_ROW_TILE = 1024
CONV_HALO = 16
FFN_CHUNK = 256
DA_TQ = 256
DA_TK = 256
DA_SUM_ROWS = 16
DA_TILES_PER_TRIP = 4
NA_ROW_GROUP = 32


def _rms(x, g):
    ms = jnp.mean(x * x, axis=-1, keepdims=True)
    return x * lax.rsqrt(ms + EPS) * g


def _params(*sem):
    return pltpu.CompilerParams(dimension_semantics=sem,
                                vmem_limit_bytes=V7X_VMEM_LIMIT_BYTES)


def _resident(shape):
    nd = len(shape)
    return pl.BlockSpec(shape, lambda *_: (0,) * nd)


def _toeplitz(v, n_rows, n_cols):
    length = n_rows + n_cols - 1
    lead = v.shape[:-1]
    flat = jnp.tile(v, (1,) * len(lead) + (n_rows + 1,))[..., :n_rows * (length + 1)]
    hankel = flat.reshape(lead + (n_rows, length + 1))[..., :n_cols]
    return hankel[..., ::-1]


def _norm_matmul_kernel(x_ref, g_ref, w_ref, o_ref, *, col_chunk):
    h = _rms(x_ref[...], g_ref[...]).astype(BF16)
    for c in range(o_ref.shape[1] // col_chunk):
        sl = slice(c * col_chunk, (c + 1) * col_chunk)
        o_ref[:, sl] = jnp.dot(h, w_ref[:, sl],
                               preferred_element_type=F32).astype(o_ref.dtype)


def _norm_matmul(x2d, g, w):
    n, d = x2d.shape
    nout = w.shape[1]
    return pl.pallas_call(
        functools.partial(_norm_matmul_kernel, col_chunk=1024),
        grid=(n // ROW_TILE,),
        in_specs=[pl.BlockSpec((ROW_TILE, d), lambda i: (i, 0)),
                  _resident((1, d)),
                  _resident((d, nout))],
        out_specs=pl.BlockSpec((ROW_TILE, nout), lambda i: (i, 0)),
        out_shape=jax.ShapeDtypeStruct((n, nout), BF16),
        compiler_params=_params("parallel"),
        name="norm_matmul",
    )(x2d, g, w)


def _da_proj_kernel(x_ref, g_ref, wk_ref, wqvt_ref, k_ref, qvt_ref):
    h = _rms(x_ref[...], g_ref[...]).astype(BF16)
    k_ref[...] = jnp.dot(h, wk_ref[...], preferred_element_type=F32).astype(k_ref.dtype)
    qvt_ref[0] = lax.dot_general(wqvt_ref[...], h, (((1,), (1,)), ((), ())),
                                 preferred_element_type=F32).astype(qvt_ref.dtype)


def _da_proj(x2d, g, wk, wqvt, batch, seq):
    n, d = x2d.shape
    tps = seq // ROW_TILE
    return pl.pallas_call(
        _da_proj_kernel,
        grid=(n // ROW_TILE,),
        in_specs=[pl.BlockSpec((ROW_TILE, d), lambda i: (i, 0)),
                  _resident((1, d)),
                  _resident(wk.shape),
                  _resident(wqvt.shape)],
        out_specs=[pl.BlockSpec((ROW_TILE, wk.shape[1]), lambda i: (i, 0)),
                   pl.BlockSpec((1, wqvt.shape[0], ROW_TILE), lambda i: (i // tps, 0, i % tps))],
        out_shape=[jax.ShapeDtypeStruct((n, wk.shape[1]), BF16),
                   jax.ShapeDtypeStruct((batch, wqvt.shape[0], seq), BF16)],
        compiler_params=_params("parallel"),
        name="da_proj",
    )(x2d, g, wk, wqvt)


def _matmul_norm_res_kernel(a_ref, w_ref, g_ref, x_ref, o_ref):
    f = jnp.dot(a_ref[...], w_ref[...], preferred_element_type=F32)
    o_ref[...] = x_ref[...] + _rms(f, g_ref[...])


def _matmul_norm_res(a2d, w, g, x2d):
    n, d = x2d.shape
    k = a2d.shape[1]
    tm = OUT_PROJ_ROW_TILE
    return pl.pallas_call(
        _matmul_norm_res_kernel,
        grid=(n // tm,),
        in_specs=[pl.BlockSpec((tm, k), lambda i: (i, 0)),
                  _resident((k, d)),
                  _resident((1, d)),
                  pl.BlockSpec((tm, d), lambda i: (i, 0))],
        out_specs=pl.BlockSpec((tm, d), lambda i: (i, 0)),
        out_shape=jax.ShapeDtypeStruct((n, d), F32),
        compiler_params=_params("parallel"),
        name="matmul_norm_res",
    )(a2d, w, g, x2d)


def _gelu_tanh(x):
    a = -2.0 * math.sqrt(2.0 / math.pi) * LOG2E
    b = a * 0.044715
    return x / (1.0 + jnp.exp2(x * (a + b * (x * x))))


def _ffn_in_kernel(xp_ref, x_ref, xn_ref, g_ref, wg_ref, wv_ref, cw_ref, cb_ref,
                   o_ref, gext_ref, val_ref, *, tiles_per_seq):
    i = pl.program_id(0)
    t = i % tiles_per_seq
    g = g_ref[...]
    has_prev = (t > 0).astype(F32)
    has_next = (t < tiles_per_seq - 1).astype(F32)
    h_prev = (_rms(xp_ref[...], g) * has_prev).astype(BF16)
    h_main = _rms(x_ref[...], g).astype(BF16)
    h_next = (_rms(xn_ref[...], g) * has_next).astype(BF16)
    h_ext = jnp.concatenate([h_prev, h_main, h_next], axis=0)
    tm = x_ref.shape[0]
    n_chunks = o_ref.shape[1] // FFN_CHUNK

    def project(c):
        sl = slice(c * FFN_CHUNK, (c + 1) * FFN_CHUNK)
        gext_ref[c % 2] = jnp.dot(h_ext, wg_ref[:, sl], preferred_element_type=F32)
        val_ref[c % 2] = jnp.dot(h_main, wv_ref[:, sl], preferred_element_type=F32)

    def activate(c):
        sl = slice(c * FFN_CHUNK, (c + 1) * FFN_CHUNK)
        gext = gext_ref[c % 2]
        cw = cw_ref[:, sl]
        rows = gext.shape[0]
        g_prev = pltpu.roll(gext, 1, axis=0)[CONV_HALO:CONV_HALO + tm]
        g_next = pltpu.roll(gext, rows - 1, axis=0)[CONV_HALO:CONV_HALO + tm]
        gate = (g_prev * cw[0:1] + gext[CONV_HALO:CONV_HALO + tm] * cw[1:2]
                + g_next * cw[2:3] + cb_ref[:, sl])
        o_ref[:, sl] = (_gelu_tanh(gate) * val_ref[c % 2]).astype(o_ref.dtype)

    project(0)
    for c in range(n_chunks):
        if c + 1 < n_chunks:
            project(c + 1)
        activate(c)


def _ffn_in(x2d, g, w_in, conv_w, conv_b, seq_len):
    n, d = x2d.shape
    nf = w_in.shape[1] // 2
    tm = FFN_IN_ROW_TILE
    hb = tm // CONV_HALO
    last = n // CONV_HALO - 1
    return pl.pallas_call(
        functools.partial(_ffn_in_kernel, tiles_per_seq=seq_len // tm),
        grid=(n // tm,),
        in_specs=[
            pl.BlockSpec((CONV_HALO, d), lambda i: (jnp.maximum(i * hb - 1, 0), 0)),
            pl.BlockSpec((tm, d), lambda i: (i, 0)),
            pl.BlockSpec((CONV_HALO, d), lambda i: (jnp.minimum((i + 1) * hb, last), 0)),
            _resident((1, d)),
            pl.BlockSpec((d, nf), lambda i: (0, 0)),
            pl.BlockSpec((d, nf), lambda i: (0, 1)),
            _resident((3, nf)),
            _resident((1, nf)),
        ],
        out_specs=pl.BlockSpec((tm, nf), lambda i: (i, 0)),
        out_shape=jax.ShapeDtypeStruct((n, nf), BF16),
        scratch_shapes=[pltpu.VMEM((2, tm + 2 * CONV_HALO, FFN_CHUNK), F32),
                        pltpu.VMEM((2, tm, FFN_CHUNK), F32)],
        compiler_params=_params("parallel"),
        name="ffn_in",
    )(x2d, x2d, x2d, g, w_in, w_in, conv_w, conv_b)


def _ffn_out_ple_kernel(a_ref, wo_ref, g3_ref, x_ref, g4_ref, wgate_ref, p_ref,
                        wproj_ref, o_ref):
    f = jnp.dot(a_ref[...], wo_ref[...], preferred_element_type=F32)
    x2 = x_ref[...] + _rms(f, g3_ref[...])
    hg = _rms(x2, g4_ref[...]).astype(BF16)
    gate = jax.nn.sigmoid(jnp.dot(hg, wgate_ref[...], preferred_element_type=F32))
    emb = jnp.dot(p_ref[...].astype(BF16), wproj_ref[...], preferred_element_type=F32)
    o_ref[...] = x2 + gate * emb


def _ffn_out_ple(a2d, w_out, g3, x2d, g4, w_gate, p2d, w_proj):
    n, d = x2d.shape
    nf = a2d.shape[1]
    pd = p2d.shape[1]
    tm = ROW_TILE
    return pl.pallas_call(
        _ffn_out_ple_kernel,
        grid=(n // tm,),
        in_specs=[pl.BlockSpec((tm, nf), lambda i: (i, 0)),
                  _resident((nf, d)),
                  _resident((1, d)),
                  pl.BlockSpec((tm, d), lambda i: (i, 0)),
                  _resident((1, d)),
                  _resident((d, d)),
                  pl.BlockSpec((tm, pd), lambda i: (i, 0)),
                  _resident((pd, d))],
        out_specs=pl.BlockSpec((tm, d), lambda i: (i, 0)),
        out_shape=jax.ShapeDtypeStruct((n, d), F32),
        compiler_params=_params("parallel"),
        name="ffn_out_ple",
    )(a2d, w_out, g3, x2d, g4, w_gate, p2d, w_proj)


def _na_kernel(q_ref, k_ref, v_ref, bias_ref, o_ref, s_ref, p_ref, *, rows):
    w = GRID_W
    nkeys = NA_KH * w
    group = s_ref.shape[0]
    lane = lax.broadcasted_iota(jnp.int32, (w, V7X_LANES), 1)
    lo = lane < (V7X_LANES // 2)

    def group_fn(gi, carry):
        geo = []
        for i in range(group):
            r = gi * group + i
            r0 = jnp.clip(r - NA_KH // 2, 0, rows - NA_KH)
            geo.append((pl.multiple_of(r * w, w), pl.multiple_of(r0 * w, w), r - r0))
        for i, (qoff, koff, _) in enumerate(geo):
            q = q_ref[0, pl.ds(qoff, w), :]
            zero = jnp.zeros_like(q)
            qs = jnp.concatenate([jnp.where(lo, q, zero), jnp.where(lo, zero, q)], axis=0)
            s_ref[i] = lax.dot_general(qs, k_ref[0, pl.ds(koff, nkeys), :],
                                       (((1,), (1,)), ((), ())),
                                       preferred_element_type=F32)
        sums = []
        for i, (_, _, d) in enumerate(geo):
            first = NA_KH - 1 - d
            bias = [jnp.concatenate([bias_ref[0, hh, first + e] for e in range(0, NA_KH, 2)],
                                    axis=1) for hh in range(2)]
            s = s_ref[i] + jnp.concatenate(bias, axis=0)
            p = jnp.exp2(s - jnp.max(s, axis=-1, keepdims=True))
            sums.append(jnp.sum(p, axis=-1, keepdims=True))
            p_ref[i] = p.astype(BF16)
        for i, (qoff, koff, _) in enumerate(geo):
            o = jnp.dot(p_ref[i], v_ref[0, pl.ds(koff, nkeys), :],
                        preferred_element_type=F32) / sums[i]
            o_ref[0, pl.ds(qoff, w), :] = jnp.where(lo, o[:w], o[w:]).astype(o_ref.dtype)
        return carry

    lax.fori_loop(0, rows // group, group_fn, 0)


def _na_bias_table(rpb):
    h = rpb.shape[0]
    w = GRID_W
    pad = w - NA_KW
    vp = jnp.pad(rpb.astype(F32), ((0, 0), (0, 0), (pad, pad)))[..., ::-1]
    tt = _toeplitz(vp, w, w)
    c = jnp.arange(w)
    kc = jnp.arange(w)
    c0 = jnp.clip(c - NA_KW // 2, 0, w - NA_KW)
    valid = (kc[None, :] >= c0[:, None]) & (kc[None, :] < c0[:, None] + NA_KW)
    tt = jnp.where(valid, tt * LOG2E, NEG_BIG)
    pairs = jnp.concatenate([tt[:, :-1], tt[:, 1:]], axis=-1)
    return pairs.reshape(h // 2, 2, 2 * NA_KH - 2, w, 2 * w)


def _neighborhood_attention(qkv, bias, batch, seq, d_model):
    hp = NA_HEADS // 2
    rows = seq // GRID_W
    blk = (1, seq, V7X_LANES)
    return pl.pallas_call(
        functools.partial(_na_kernel, rows=rows),
        grid=(hp, batch),
        in_specs=[pl.BlockSpec(blk, lambda h, b: (b, 0, h)),
                  pl.BlockSpec(blk, lambda h, b: (b, 0, hp + h)),
                  pl.BlockSpec(blk, lambda h, b: (b, 0, 2 * hp + h)),
                  pl.BlockSpec((1,) + bias.shape[1:], lambda h, b: (h, 0, 0, 0, 0))],
        out_specs=pl.BlockSpec(blk, lambda h, b: (b, 0, h)),
        out_shape=jax.ShapeDtypeStruct((batch, seq, d_model), BF16),
        scratch_shapes=[pltpu.VMEM((NA_ROW_GROUP, 2 * GRID_W, NA_KH * GRID_W), F32),
                        pltpu.VMEM((NA_ROW_GROUP, 2 * GRID_W, NA_KH * GRID_W), BF16)],
        compiler_params=_params("parallel", "parallel"),
        name="neighborhood_attention",
    )(qkv, qkv, qkv, bias)


def _t5_bucket(rel):
    half = T5_BUCKETS // 2
    max_exact = half // 2
    sign_off = jnp.where(rel > 0, half, 0)
    n = jnp.abs(rel)
    nf = jnp.maximum(n, 1).astype(F32)
    large = max_exact + (jnp.log(nf / max_exact) / math.log(T5_MAX_DIST / max_exact)
                         * (half - max_exact)).astype(jnp.int32)
    large = jnp.minimum(large, half - 1)
    return sign_off + jnp.where(n < max_exact, n, large)


DA_BLK = V7X_LANES
DA_BLK_SPAN = 2
assert (DA_BLK_SPAN - 1) * DA_BLK + 1 >= T5_MAX_DIST


def _da_bias_blocks(t5_table):
    span = DA_BLK_SPAN * DA_BLK + DA_BLK - 1
    rel = jnp.arange(-span, span + 1)
    vals = (t5_table[_t5_bucket(rel)].astype(F32) * LOG2E).T
    vecs = jnp.stack([lax.slice_in_dim(vals, DA_BLK * d - (DA_BLK - 1) + span,
                                       DA_BLK * d + DA_BLK + span, axis=1)
                      for d in range(-DA_BLK_SPAN, DA_BLK_SPAN + 1)], axis=1)
    blocks = _toeplitz(vecs, DA_BLK, DA_BLK)
    far = jnp.stack([vals[:, 0], vals[:, -1]], axis=1)
    return blocks, far


def _da_kernel(far_ref, qt_ref, k_ref, vt_ref, blk_ref, lam_ref, g_ref, o_ref,
               qst_ref, vte_ref, st_ref, p_ref, m_ref, alpha_ref, acc_ref,
               *, lambda_init, n_kt, n_qt):
    h = pl.program_id(0)
    tq, tk = DA_TQ, DA_TK
    nv = V7X_LANES
    half = V7X_LANES // 2
    c_left = far_ref[h, 0]
    c_right = far_ref[h, 1]

    vte_ref[0:nv, :] = vt_ref[0]
    vte_ref[nv:, :] = jnp.ones((DA_SUM_ROWS, vte_ref.shape[1]), BF16)

    lam = lam_ref[...]
    lam_full = (jnp.exp(jnp.sum(lam[0:1] * lam[1:2], axis=-1, keepdims=True))
                - jnp.exp(jnp.sum(lam[2:3] * lam[3:4], axis=-1, keepdims=True))
                + lambda_init)

    def first_tile(qi):
        return (qi * tq + tk - (T5_MAX_DIST - 1)) // tk - 1

    n_band = max((q0 + tq - 1 + T5_MAX_DIST - 1) // tk - (q0 - (T5_MAX_DIST - 1)) // tk + 1
                 for q0 in range(0, n_qt * tq, tq))

    def tile_of(qi, slot):
        return (first_tile(qi) + slot) & (n_kt - 1)

    def scores(qi, slot, bias=None):
        off = pl.multiple_of(tile_of(qi, slot) * tk, tk)
        st = jnp.dot(k_ref[0, pl.ds(off, tk), :], qst_ref[...], preferred_element_type=F32)
        if bias is not None:
            st = st + jnp.concatenate([bias, bias], axis=1)
        st_ref[slot % 2] = st

    def accumulate(qi, slot):
        off = pl.multiple_of(tile_of(qi, slot) * tk, tk)
        acc_ref[...] = alpha_ref[...] * acc_ref[...] + jnp.dot(
            vte_ref[:, pl.ds(off, tk)], p_ref[slot % 2], preferred_element_type=F32)

    def softmax(slot, c):
        st = st_ref[slot % 2]
        m_prev = m_ref[...]
        m_next = jnp.maximum(m_prev, jnp.max(st, axis=0, keepdims=True) + c)
        p_ref[slot % 2] = jnp.exp2(st - (m_next - c)).astype(BF16)
        alpha_ref[...] = jnp.exp2(m_prev - m_next)
        m_ref[...] = m_next

    def full_bias(qi, slot):
        j = tile_of(qi, slot)
        rows = []
        for kb in range(tk // DA_BLK):
            d = [(tk // DA_BLK) * j + kb - (tq // DA_BLK) * qi - qb
                 for qb in range(tq // DA_BLK)]
            rows.append(jnp.concatenate(
                [blk_ref[0, jnp.clip(x, -DA_BLK_SPAN, DA_BLK_SPAN) + DA_BLK_SPAN] for x in d],
                axis=1))
        return jnp.concatenate(rows, axis=0)

    def start(qi):
        feat = lax.broadcasted_iota(jnp.int32, (V7X_LANES, tq), 0)
        qt = qt_ref[0, :, pl.ds(pl.multiple_of(qi * tq, tq), tq)]
        zero = jnp.zeros_like(qt)
        qst_ref[:, 0:tq] = jnp.where(feat < half, qt, zero)
        qst_ref[:, tq:2 * tq] = jnp.where(feat < half, zero, qt)
        scores(qi, 0, full_bias(qi, 0))

    def run_slots(qi):
        m_ref[...] = jnp.full(m_ref.shape, NEG_BIG, F32)
        alpha_ref[...] = jnp.zeros(alpha_ref.shape, F32)
        acc_ref[...] = jnp.zeros(acc_ref.shape, F32)
        jn = first_tile(qi)

        def far_c(slot):
            return jnp.where(tile_of(qi, slot) > jn, c_right, c_left)

        def bias_of(slot):
            return full_bias(qi, slot) if slot < n_band else None

        scores(qi, 1, bias_of(1))
        softmax(0, 0.0)
        for s in range(1, n_kt - 1):
            scores(qi, s + 1, bias_of(s + 1))
            accumulate(qi, s - 1)
            softmax(s, 0.0 if s < n_band else far_c(s))
        accumulate(qi, n_kt - 2)
        softmax(n_kt - 1, far_c(n_kt - 1))
        accumulate(qi, n_kt - 1)

    def finish(qi):
        o_all = acc_ref[0:nv, :] / acc_ref[nv:nv + 1, :]
        o_t = o_all[:, :tq] - lam_full * o_all[:, tq:]
        o = _rms(o_t.T, g_ref[...]) * (1.0 - lambda_init)
        o_ref[0, pl.ds(pl.multiple_of(qi * tq, tq), tq), :] = o.astype(o_ref.dtype)

    def q_group(gi, carry):
        first = gi * DA_TILES_PER_TRIP
        start(first)
        for t in range(DA_TILES_PER_TRIP):
            run_slots(first + t)
            if t + 1 < DA_TILES_PER_TRIP:
                start(first + t + 1)
            finish(first + t)
        return carry

    lax.fori_loop(0, n_qt // DA_TILES_PER_TRIP, q_group, 0)


def _diff_attention(k, qvt, blocks, far, lam, subln_g, lambda_init, batch, seq, d_model):
    nh = DA_HEADS
    n_qt = seq // DA_TQ
    n_kt = seq // DA_TK
    assert n_kt & (n_kt - 1) == 0
    return pl.pallas_call(
        functools.partial(_da_kernel, lambda_init=lambda_init, n_kt=n_kt, n_qt=n_qt),
        grid=(nh, batch),
        in_specs=[pl.BlockSpec(memory_space=pltpu.SMEM),
                  pl.BlockSpec((1, V7X_LANES, seq), lambda h, b: (b, h, 0)),
                  pl.BlockSpec((1, seq, V7X_LANES), lambda h, b: (b, 0, h)),
                  pl.BlockSpec((1, V7X_LANES, seq), lambda h, b: (b, nh + h, 0)),
                  pl.BlockSpec((1,) + blocks.shape[1:], lambda h, b: (h, 0, 0, 0)),
                  _resident(lam.shape),
                  _resident(subln_g.shape)],
        out_specs=pl.BlockSpec((1, seq, V7X_LANES), lambda h, b: (b, 0, h)),
        out_shape=jax.ShapeDtypeStruct((batch, seq, d_model), BF16),
        scratch_shapes=[pltpu.VMEM((V7X_LANES, 2 * DA_TQ), BF16),
                        pltpu.VMEM((V7X_LANES + DA_SUM_ROWS, seq), BF16),
                        pltpu.VMEM((2, DA_TK, 2 * DA_TQ), F32),
                        pltpu.VMEM((2, DA_TK, 2 * DA_TQ), BF16),
                        pltpu.VMEM((1, 2 * DA_TQ), F32),
                        pltpu.VMEM((1, 2 * DA_TQ), F32),
                        pltpu.VMEM((V7X_LANES + DA_SUM_ROWS, 2 * DA_TQ), F32)],
        compiler_params=_params("parallel", "parallel"),
        name="diff_attention",
    )(far, qvt, k, qvt, blocks, lam, subln_g)


def kernel(x, p, norm_g, na_w_qkv, na_rpb, na_w_o, da_w_qkv, da_lambda, da_subln_g,
           da_w_o, t5_table, ffn_w_in, ffn_conv_w, ffn_conv_b, ffn_w_out,
           ple_w_gate, ple_w_proj):
    batch, seq, d_model = x.shape
    depth = norm_g.shape[0]
    n_tok = batch * seq
    x2d = x.reshape(n_tok, d_model)

    for i in range(depth):
        g = norm_g[i].astype(F32)
        j = i // N_MIXERS
        if i % N_MIXERS == 0:
            w = na_w_qkv[j]
            dh = d_model // NA_HEADS
            w = jnp.concatenate([w[:, :d_model] * (LOG2E * dh ** -0.5), w[:, d_model:]], axis=1)
            qkv = _norm_matmul(x2d, g[0:1], w.astype(BF16))
            att = _neighborhood_attention(qkv.reshape(batch, seq, 3 * d_model),
                                          _na_bias_table(na_rpb[j]), batch, seq, d_model)
            w_o = na_w_o[j]
        else:
            lambda_init = 0.8 - 0.6 * math.exp(-0.3 * i)
            w = da_w_qkv[j]
            dh = d_model // (2 * DA_HEADS)
            wqvt = jnp.concatenate([w[:, :d_model] * (LOG2E * dh ** -0.5),
                                    w[:, 2 * d_model:]], axis=1).T.astype(BF16)
            k, qvt = _da_proj(x2d, g[0:1], w[:, d_model:2 * d_model].astype(BF16), wqvt,
                              batch, seq)
            blocks, far = _da_bias_blocks(t5_table)
            att = _diff_attention(k.reshape(batch, seq, d_model), qvt, blocks, far,
                                  da_lambda[j].astype(F32),
                                  da_subln_g[j].astype(F32)[None, :],
                                  lambda_init, batch, seq, d_model)
            w_o = da_w_o[j]
        x2d = _matmul_norm_res(att.reshape(n_tok, d_model), w_o.astype(BF16), g[1:2], x2d)
        w_in = ffn_w_in[i].astype(BF16)
        act = _ffn_in(x2d, g[2:3], w_in,
                      ffn_conv_w[i].astype(F32), ffn_conv_b[i].astype(F32)[None, :], seq)
        x2d = _ffn_out_ple(act, ffn_w_out[i].astype(BF16), g[3:4], x2d, g[4:5],
                           ple_w_gate[i].astype(BF16),
                           p[i].reshape(n_tok, p.shape[-1]),
                           ple_w_proj[i].astype(BF16))
    return x2d.reshape(batch, seq, d_model)
```

```python
import functools
import math

import jax
import jax.numpy as jnp
from jax import lax
from jax.experimental import pallas as pl
from jax.experimental.pallas import tpu as pltpu

F32 = jnp.float32
BF16 = jnp.bfloat16

GRID_W = 64
NA_HEADS = 16
NA_KH = 8
NA_KW = 16
DA_HEADS = 8
T5_BUCKETS = 32
T5_MAX_DIST = 128
EPS = 1e-6
N_MIXERS = 2

V7X_LANES = 128
V7X_VMEM_LIMIT_BYTES = 56 * 1024 * 1024

NEG_BIG = -1e30
LOG2E = math.log2(math.e)

ROW_TILE = 1024
FFN_IN_ROW_TILE = 1024
FFN_OUT_ROW_TILE = 1024
OUT_PROJ_ROW_TILE = 1024
CONV_HALO = 16
FFN_CHUNK = 256
DA_TQ = 256
DA_TK = 256
DA_SUM_ROWS = 16
DA_TILES_PER_TRIP = 8
NA_ROW_GROUP = 32


def _rms(x, g):
    ms = jnp.mean(x * x, axis=-1, keepdims=True)
    return x * lax.rsqrt(ms + EPS) * g


def _params(*sem):
    return pltpu.CompilerParams(dimension_semantics=sem,
                                vmem_limit_bytes=V7X_VMEM_LIMIT_BYTES)


def _resident(shape):
    nd = len(shape)
    return pl.BlockSpec(shape, lambda *_: (0,) * nd, pipeline_mode=pl.Buffered(1))


def _toeplitz(v, n_rows, n_cols):
    length = n_rows + n_cols - 1
    lead = v.shape[:-1]
    flat = jnp.tile(v, (1,) * len(lead) + (n_rows + 1,))[..., :n_rows * (length + 1)]
    hankel = flat.reshape(lead + (n_rows, length + 1))[..., :n_cols]
    return hankel[..., ::-1]


def _norm_matmul_kernel(x_ref, g_ref, w_ref, o_ref, *, col_chunk):
    h = _rms(x_ref[...], g_ref[...]).astype(BF16)
    for c in range(o_ref.shape[1] // col_chunk):
        sl = slice(c * col_chunk, (c + 1) * col_chunk)
        o_ref[:, sl] = jnp.dot(h, w_ref[:, sl],
                               preferred_element_type=F32).astype(o_ref.dtype)


def _norm_matmul(x2d, g, w):
    n, d = x2d.shape
    nout = w.shape[1]
    return pl.pallas_call(
        functools.partial(_norm_matmul_kernel, col_chunk=1024),
        grid=(n // ROW_TILE,),
        in_specs=[pl.BlockSpec((ROW_TILE, d), lambda i: (i, 0)),
                  _resident((1, d)),
                  _resident((d, nout))],
        out_specs=pl.BlockSpec((ROW_TILE, nout), lambda i: (i, 0)),
        out_shape=jax.ShapeDtypeStruct((n, nout), BF16),
        compiler_params=_params("parallel"),
        name="norm_matmul",
    )(x2d, g, w)


def _da_proj_kernel(x_ref, g_ref, wk_ref, wqvt_ref, k_ref, qvt_ref):
    h = _rms(x_ref[...], g_ref[...]).astype(BF16)
    k_ref[...] = jnp.dot(h, wk_ref[...], preferred_element_type=F32).astype(k_ref.dtype)
    qvt_ref[0] = lax.dot_general(wqvt_ref[...], h, (((1,), (1,)), ((), ())),
                                 preferred_element_type=F32).astype(qvt_ref.dtype)


def _da_proj(x2d, g, wk, wqvt, batch, seq):
    n, d = x2d.shape
    tps = seq // ROW_TILE
    return pl.pallas_call(
        _da_proj_kernel,
        grid=(n // ROW_TILE,),
        in_specs=[pl.BlockSpec((ROW_TILE, d), lambda i: (i, 0)),
                  _resident((1, d)),
                  _resident(wk.shape),
                  _resident(wqvt.shape)],
        out_specs=[pl.BlockSpec((ROW_TILE, wk.shape[1]), lambda i: (i, 0)),
                   pl.BlockSpec((1, wqvt.shape[0], ROW_TILE), lambda i: (i // tps, 0, i % tps))],
        out_shape=[jax.ShapeDtypeStruct((n, wk.shape[1]), BF16),
                   jax.ShapeDtypeStruct((batch, wqvt.shape[0], seq), BF16)],
        compiler_params=_params("parallel"),
        name="da_proj",
    )(x2d, g, wk, wqvt)


def _matmul_norm_res_kernel(a_ref, w_ref, g_ref, x_ref, o_ref):
    f = jnp.dot(a_ref[...], w_ref[...], preferred_element_type=F32)
    o_ref[...] = x_ref[...] + _rms(f, g_ref[...])


def _matmul_norm_res(a2d, w, g, x2d):
    n, d = x2d.shape
    k = a2d.shape[1]
    tm = OUT_PROJ_ROW_TILE
    return pl.pallas_call(
        _matmul_norm_res_kernel,
        grid=(n // tm,),
        in_specs=[pl.BlockSpec((tm, k), lambda i: (i, 0)),
                  _resident((k, d)),
                  _resident((1, d)),
                  pl.BlockSpec((tm, d), lambda i: (i, 0))],
        out_specs=pl.BlockSpec((tm, d), lambda i: (i, 0)),
        out_shape=jax.ShapeDtypeStruct((n, d), F32),
        compiler_params=_params("parallel"),
        name="matmul_norm_res",
    )(a2d, w, g, x2d)


def _gelu_tanh(x):
    a = -2.0 * math.sqrt(2.0 / math.pi) * LOG2E
    b = a * 0.044715
    return x / (1.0 + jnp.exp2(x * (a + b * (x * x))))


def _ffn_in_kernel(xp_ref, x_ref, xn_ref, g_ref, wg_ref, wv_ref, cw_ref, cb_ref,
                   o_ref, gext_ref, val_ref, *, tiles_per_seq):
    i = pl.program_id(0)
    t = i % tiles_per_seq
    g = g_ref[...]
    has_prev = (t > 0).astype(F32)
    has_next = (t < tiles_per_seq - 1).astype(F32)
    h_prev = (_rms(xp_ref[...], g) * has_prev).astype(BF16)
    h_main = _rms(x_ref[...], g).astype(BF16)
    h_next = (_rms(xn_ref[...], g) * has_next).astype(BF16)
    h_ext = jnp.concatenate([h_prev, h_main, h_next], axis=0)
    tm = x_ref.shape[0]
    n_chunks = o_ref.shape[1] // FFN_CHUNK

    def project(c):
        sl = slice(c * FFN_CHUNK, (c + 1) * FFN_CHUNK)
        gext_ref[c % 2] = jnp.dot(h_ext, wg_ref[:, sl], preferred_element_type=F32)
        val_ref[c % 2] = jnp.dot(h_main, wv_ref[:, sl], preferred_element_type=F32)

    def activate(c):
        sl = slice(c * FFN_CHUNK, (c + 1) * FFN_CHUNK)
        gext = gext_ref[c % 2]
        cw = cw_ref[:, sl]
        rows = gext.shape[0]
        g_prev = pltpu.roll(gext, 1, axis=0)[CONV_HALO:CONV_HALO + tm]
        g_next = pltpu.roll(gext, rows - 1, axis=0)[CONV_HALO:CONV_HALO + tm]
        gate = (g_prev * cw[0:1] + gext[CONV_HALO:CONV_HALO + tm] * cw[1:2]
                + g_next * cw[2:3] + cb_ref[:, sl])
        o_ref[:, sl] = (_gelu_tanh(gate) * val_ref[c % 2]).astype(o_ref.dtype)

    project(0)
    for c in range(n_chunks):
        if c + 1 < n_chunks:
            project(c + 1)
        activate(c)


def _ffn_in(x2d, g, w_in, conv_w, conv_b, seq_len):
    n, d = x2d.shape
    nf = w_in.shape[1] // 2
    tm = FFN_IN_ROW_TILE
    hb = tm // CONV_HALO
    last = n // CONV_HALO - 1
    return pl.pallas_call(
        functools.partial(_ffn_in_kernel, tiles_per_seq=seq_len // tm),
        grid=(n // tm,),
        in_specs=[
            pl.BlockSpec((CONV_HALO, d), lambda i: (jnp.maximum(i * hb - 1, 0), 0)),
            pl.BlockSpec((tm, d), lambda i: (i, 0)),
            pl.BlockSpec((CONV_HALO, d), lambda i: (jnp.minimum((i + 1) * hb, last), 0)),
            _resident((1, d)),
            pl.BlockSpec((d, nf), lambda i: (0, 0)),
            pl.BlockSpec((d, nf), lambda i: (0, 1)),
            _resident((3, nf)),
            _resident((1, nf)),
        ],
        out_specs=pl.BlockSpec((tm, nf), lambda i: (i, 0)),
        out_shape=jax.ShapeDtypeStruct((n, nf), BF16),
        scratch_shapes=[pltpu.VMEM((2, tm + 2 * CONV_HALO, FFN_CHUNK), F32),
                        pltpu.VMEM((2, tm, FFN_CHUNK), F32)],
        compiler_params=_params("parallel"),
        name="ffn_in",
    )(x2d, x2d, x2d, g, w_in, w_in, conv_w, conv_b)


def _ffn_out_ple_kernel(a_ref, wo_ref, g3_ref, x_ref, g4_ref, wgate_ref, p_ref,
                        wproj_ref, o_ref):
    f = jnp.dot(a_ref[...], wo_ref[...], preferred_element_type=F32)
    x2 = x_ref[...] + _rms(f, g3_ref[...])
    hg = _rms(x2, g4_ref[...]).astype(BF16)
    gate = jax.nn.sigmoid(jnp.dot(hg, wgate_ref[...], preferred_element_type=F32))
    emb = jnp.dot(p_ref[...].astype(BF16), wproj_ref[...], preferred_element_type=F32)
    o_ref[...] = x2 + gate * emb


def _ffn_out_ple(a2d, w_out, g3, x2d, g4, w_gate, p2d, w_proj):
    n, d = x2d.shape
    nf = a2d.shape[1]
    pd = p2d.shape[1]
    tm = FFN_OUT_ROW_TILE
    return pl.pallas_call(
        _ffn_out_ple_kernel,
        grid=(n // tm,),
        in_specs=[pl.BlockSpec((tm, nf), lambda i: (i, 0)),
                  _resident((nf, d)),
                  _resident((1, d)),
                  pl.BlockSpec((tm, d), lambda i: (i, 0)),
                  _resident((1, d)),
                  _resident((d, d)),
                  pl.BlockSpec((tm, pd), lambda i: (i, 0)),
                  _resident((pd, d))],
        out_specs=pl.BlockSpec((tm, d), lambda i: (i, 0)),
        out_shape=jax.ShapeDtypeStruct((n, d), F32),
        compiler_params=_params("parallel"),
        name="ffn_out_ple",
    )(a2d, w_out, g3, x2d, g4, w_gate, p2d, w_proj)


def _na_kernel(q_ref, k_ref, v_ref, bias_ref, o_ref, s_ref, p_ref, *, rows):
    w = GRID_W
    nkeys = NA_KH * w
    group = s_ref.shape[0]
    lane = lax.broadcasted_iota(jnp.int32, (w, V7X_LANES), 1)
    lo = lane < (V7X_LANES // 2)

    def group_fn(gi, carry):
        geo = []
        for i in range(group):
            r = gi * group + i
            r0 = jnp.clip(r - NA_KH // 2, 0, rows - NA_KH)
            geo.append((pl.multiple_of(r * w, w), pl.multiple_of(r0 * w, w), r - r0))
        for i, (qoff, koff, _) in enumerate(geo):
            q = q_ref[0, pl.ds(qoff, w), :]
            zero = jnp.zeros_like(q)
            qs = jnp.concatenate([jnp.where(lo, q, zero), jnp.where(lo, zero, q)], axis=0)
            s_ref[i] = lax.dot_general(qs, k_ref[0, pl.ds(koff, nkeys), :],
                                       (((1,), (1,)), ((), ())),
                                       preferred_element_type=F32)
        sums = []
        for i, (_, _, d) in enumerate(geo):
            first = NA_KH - 1 - d
            bias = [jnp.concatenate([bias_ref[0, hh, first + e] for e in range(0, NA_KH, 2)],
                                    axis=1) for hh in range(2)]
            s = s_ref[i] + jnp.concatenate(bias, axis=0)
            p = jnp.exp2(s - jnp.max(s, axis=-1, keepdims=True))
            sums.append(jnp.sum(p, axis=-1, keepdims=True))
            p_ref[i] = p.astype(BF16)
        for i, (qoff, koff, _) in enumerate(geo):
            o = jnp.dot(p_ref[i], v_ref[0, pl.ds(koff, nkeys), :],
                        preferred_element_type=F32) / sums[i]
            o_ref[0, pl.ds(qoff, w), :] = jnp.where(lo, o[:w], o[w:]).astype(o_ref.dtype)
        return carry

    lax.fori_loop(0, rows // group, group_fn, 0)


def _na_bias_table(rpb):
    h = rpb.shape[0]
    w = GRID_W
    pad = w - NA_KW
    vp = jnp.pad(rpb.astype(F32), ((0, 0), (0, 0), (pad, pad)))[..., ::-1]
    tt = _toeplitz(vp, w, w)
    c = jnp.arange(w)
    kc = jnp.arange(w)
    c0 = jnp.clip(c - NA_KW // 2, 0, w - NA_KW)
    valid = (kc[None, :] >= c0[:, None]) & (kc[None, :] < c0[:, None] + NA_KW)
    tt = jnp.where(valid, tt * LOG2E, NEG_BIG)
    pairs = jnp.concatenate([tt[:, :-1], tt[:, 1:]], axis=-1)
    return pairs.reshape(h // 2, 2, 2 * NA_KH - 2, w, 2 * w)


def _neighborhood_attention(qkv, bias, batch, seq, d_model):
    hp = NA_HEADS // 2
    rows = seq // GRID_W
    blk = (1, seq, V7X_LANES)
    return pl.pallas_call(
        functools.partial(_na_kernel, rows=rows),
        grid=(hp, batch),
        in_specs=[pl.BlockSpec(blk, lambda h, b: (b, 0, h)),
                  pl.BlockSpec(blk, lambda h, b: (b, 0, hp + h)),
                  pl.BlockSpec(blk, lambda h, b: (b, 0, 2 * hp + h)),
                  pl.BlockSpec((1,) + bias.shape[1:], lambda h, b: (h, 0, 0, 0, 0))],
        out_specs=pl.BlockSpec(blk, lambda h, b: (b, 0, h)),
        out_shape=jax.ShapeDtypeStruct((batch, seq, d_model), BF16),
        scratch_shapes=[pltpu.VMEM((NA_ROW_GROUP, 2 * GRID_W, NA_KH * GRID_W), F32),
                        pltpu.VMEM((NA_ROW_GROUP, 2 * GRID_W, NA_KH * GRID_W), BF16)],
        compiler_params=_params("parallel", "parallel"),
        name="neighborhood_attention",
    )(qkv, qkv, qkv, bias)


def _t5_bucket(rel):
    half = T5_BUCKETS // 2
    max_exact = half // 2
    sign_off = jnp.where(rel > 0, half, 0)
    n = jnp.abs(rel)
    nf = jnp.maximum(n, 1).astype(F32)
    large = max_exact + (jnp.log(nf / max_exact) / math.log(T5_MAX_DIST / max_exact)
                         * (half - max_exact)).astype(jnp.int32)
    large = jnp.minimum(large, half - 1)
    return sign_off + jnp.where(n < max_exact, n, large)


DA_BLK = V7X_LANES
DA_BLK_SPAN = 2
assert (DA_BLK_SPAN - 1) * DA_BLK + 1 >= T5_MAX_DIST


def _da_bias_blocks(t5_table):
    span = DA_BLK_SPAN * DA_BLK + DA_BLK - 1
    rel = jnp.arange(-span, span + 1)
    vals = (t5_table[_t5_bucket(rel)].astype(F32) * LOG2E).T
    vecs = jnp.stack([lax.slice_in_dim(vals, DA_BLK * d - (DA_BLK - 1) + span,
                                       DA_BLK * d + DA_BLK + span, axis=1)
                      for d in range(-DA_BLK_SPAN, DA_BLK_SPAN + 1)], axis=1)
    blocks = _toeplitz(vecs, DA_BLK, DA_BLK)
    far = jnp.stack([vals[:, 0], vals[:, -1]], axis=1)
    return blocks, far


def _da_kernel(far_ref, qt_ref, k_ref, vt_ref, blk_ref, lam_ref, g_ref, o_ref,
               qst_ref, vte_ref, st_ref, p_ref, m_ref, alpha_ref, acc_ref,
               *, lambda_init, n_kt, n_qt):
    h = pl.program_id(0)
    tq, tk = DA_TQ, DA_TK
    nv = V7X_LANES
    half = V7X_LANES // 2
    c_left = far_ref[h, 0]
    c_right = far_ref[h, 1]

    vte_ref[0:nv, :] = vt_ref[0]
    vte_ref[nv:, :] = jnp.ones((DA_SUM_ROWS, vte_ref.shape[1]), BF16)

    lam = lam_ref[...]
    lam_full = (jnp.exp(jnp.sum(lam[0:1] * lam[1:2], axis=-1, keepdims=True))
                - jnp.exp(jnp.sum(lam[2:3] * lam[3:4], axis=-1, keepdims=True))
                + lambda_init)

    def first_tile(qi):
        return (qi * tq + tk - (T5_MAX_DIST - 1)) // tk - 1

    n_band = max((q0 + tq - 1 + T5_MAX_DIST - 1) // tk - (q0 - (T5_MAX_DIST - 1)) // tk + 1
                 for q0 in range(0, n_qt * tq, tq))

    def tile_of(qi, slot):
        return (first_tile(qi) + slot) & (n_kt - 1)

    def scores(qi, slot, bias=None):
        off = pl.multiple_of(tile_of(qi, slot) * tk, tk)
        st = jnp.dot(k_ref[0, pl.ds(off, tk), :], qst_ref[...], preferred_element_type=F32)
        if bias is not None:
            st = st + jnp.concatenate([bias, bias], axis=1)
        st_ref[slot % 2] = st

    def accumulate(qi, slot):
        off = pl.multiple_of(tile_of(qi, slot) * tk, tk)
        acc_ref[...] = alpha_ref[...] * acc_ref[...] + jnp.dot(
            vte_ref[:, pl.ds(off, tk)], p_ref[slot % 2], preferred_element_type=F32)

    def softmax(slot, c):
        st = st_ref[slot % 2]
        m_prev = m_ref[...]
        m_next = jnp.maximum(m_prev, jnp.max(st, axis=0, keepdims=True) + c)
        p_ref[slot % 2] = jnp.exp2(st - (m_next - c)).astype(BF16)
        alpha_ref[...] = jnp.exp2(m_prev - m_next)
        m_ref[...] = m_next

    def full_bias(qi, slot):
        j = tile_of(qi, slot)
        rows = []
        for kb in range(tk // DA_BLK):
            d = [(tk // DA_BLK) * j + kb - (tq // DA_BLK) * qi - qb
                 for qb in range(tq // DA_BLK)]
            rows.append(jnp.concatenate(
                [blk_ref[0, jnp.clip(x, -DA_BLK_SPAN, DA_BLK_SPAN) + DA_BLK_SPAN] for x in d],
                axis=1))
        return jnp.concatenate(rows, axis=0)

    def start(qi):
        feat = lax.broadcasted_iota(jnp.int32, (V7X_LANES, tq), 0)
        qt = qt_ref[0, :, pl.ds(pl.multiple_of(qi * tq, tq), tq)]
        zero = jnp.zeros_like(qt)
        qst_ref[:, 0:tq] = jnp.where(feat < half, qt, zero)
        qst_ref[:, tq:2 * tq] = jnp.where(feat < half, zero, qt)
        scores(qi, 0, full_bias(qi, 0))

    def run_slots(qi):
        m_ref[...] = jnp.full(m_ref.shape, NEG_BIG, F32)
        alpha_ref[...] = jnp.zeros(alpha_ref.shape, F32)
        acc_ref[...] = jnp.zeros(acc_ref.shape, F32)
        jn = first_tile(qi)

        def far_c(slot):
            return jnp.where(tile_of(qi, slot) > jn, c_right, c_left)

        def bias_of(slot):
            return full_bias(qi, slot) if slot < n_band else None

        scores(qi, 1, bias_of(1))
        softmax(0, 0.0)
        for s in range(1, n_kt - 1):
            scores(qi, s + 1, bias_of(s + 1))
            accumulate(qi, s - 1)
            softmax(s, 0.0 if s < n_band else far_c(s))
        accumulate(qi, n_kt - 2)
        softmax(n_kt - 1, far_c(n_kt - 1))
        accumulate(qi, n_kt - 1)

    def finish(qi):
        o_all = acc_ref[0:nv, :] / acc_ref[nv:nv + 1, :]
        o_t = o_all[:, :tq] - lam_full * o_all[:, tq:]
        o = _rms(o_t.T, g_ref[...]) * (1.0 - lambda_init)
        o_ref[0, pl.ds(pl.multiple_of(qi * tq, tq), tq), :] = o.astype(o_ref.dtype)

    def q_group(gi, carry):
        first = gi * DA_TILES_PER_TRIP
        start(first)
        for t in range(DA_TILES_PER_TRIP):
            run_slots(first + t)
            if t + 1 < DA_TILES_PER_TRIP:
                start(first + t + 1)
            finish(first + t)
        return carry

    lax.fori_loop(0, n_qt // DA_TILES_PER_TRIP, q_group, 0)


def _diff_attention(k, qvt, blocks, far, lam, subln_g, lambda_init, batch, seq, d_model):
    nh = DA_HEADS
    n_qt = seq // DA_TQ
    n_kt = seq // DA_TK
    assert n_kt & (n_kt - 1) == 0
    return pl.pallas_call(
        functools.partial(_da_kernel, lambda_init=lambda_init, n_kt=n_kt, n_qt=n_qt),
        grid=(nh, batch),
        in_specs=[pl.BlockSpec(memory_space=pltpu.SMEM),
                  pl.BlockSpec((1, V7X_LANES, seq), lambda h, b: (b, h, 0)),
                  pl.BlockSpec((1, seq, V7X_LANES), lambda h, b: (b, 0, h)),
                  pl.BlockSpec((1, V7X_LANES, seq), lambda h, b: (b, nh + h, 0)),
                  pl.BlockSpec((1,) + blocks.shape[1:], lambda h, b: (h, 0, 0, 0)),
                  _resident(lam.shape),
                  _resident(subln_g.shape)],
        out_specs=pl.BlockSpec((1, seq, V7X_LANES), lambda h, b: (b, 0, h)),
        out_shape=jax.ShapeDtypeStruct((batch, seq, d_model), BF16),
        scratch_shapes=[pltpu.VMEM((V7X_LANES, 2 * DA_TQ), BF16),
                        pltpu.VMEM((V7X_LANES + DA_SUM_ROWS, seq), BF16),
                        pltpu.VMEM((2, DA_TK, 2 * DA_TQ), F32),
                        pltpu.VMEM((2, DA_TK, 2 * DA_TQ), BF16),
                        pltpu.VMEM((1, 2 * DA_TQ), F32),
                        pltpu.VMEM((1, 2 * DA_TQ), F32),
                        pltpu.VMEM((V7X_LANES + DA_SUM_ROWS, 2 * DA_TQ), F32)],
        compiler_params=_params("parallel", "parallel"),
        name="diff_attention",
    )(far, qvt, k, qvt, blocks, lam, subln_g)


def kernel(x, p, norm_g, na_w_qkv, na_rpb, na_w_o, da_w_qkv, da_lambda, da_subln_g,
           da_w_o, t5_table, ffn_w_in, ffn_conv_w, ffn_conv_b, ffn_w_out,
           ple_w_gate, ple_w_proj):
    batch, seq, d_model = x.shape
    depth = norm_g.shape[0]
    n_tok = batch * seq
    x2d = x.reshape(n_tok, d_model)

    for i in range(depth):
        g = norm_g[i].astype(F32)
        j = i // N_MIXERS
        if i % N_MIXERS == 0:
            w = na_w_qkv[j]
            dh = d_model // NA_HEADS
            w = jnp.concatenate([w[:, :d_model] * (LOG2E * dh ** -0.5), w[:, d_model:]], axis=1)
            qkv = _norm_matmul(x2d, g[0:1], w.astype(BF16))
            att = _neighborhood_attention(qkv.reshape(batch, seq, 3 * d_model),
                                          _na_bias_table(na_rpb[j]), batch, seq, d_model)
            w_o = na_w_o[j]
        else:
            lambda_init = 0.8 - 0.6 * math.exp(-0.3 * i)
            w = da_w_qkv[j]
            dh = d_model // (2 * DA_HEADS)
            wqvt = jnp.concatenate([w[:, :d_model] * (LOG2E * dh ** -0.5),
                                    w[:, 2 * d_model:]], axis=1).T.astype(BF16)
            k, qvt = _da_proj(x2d, g[0:1], w[:, d_model:2 * d_model].astype(BF16), wqvt,
                              batch, seq)
            blocks, far = _da_bias_blocks(t5_table)
            att = _diff_attention(k.reshape(batch, seq, d_model), qvt, blocks, far,
                                  da_lambda[j].astype(F32),
                                  da_subln_g[j].astype(F32)[None, :],
                                  lambda_init, batch, seq, d_model)
            w_o = da_w_o[j]
        x2d = _matmul_norm_res(att.reshape(n_tok, d_model), w_o.astype(BF16), g[1:2], x2d)
        w_in = ffn_w_in[i].astype(BF16)
        act = _ffn_in(x2d, g[2:3], w_in,
                      ffn_conv_w[i].astype(F32), ffn_conv_b[i].astype(F32)[None, :], seq)
        x2d = _ffn_out_ple(act, ffn_w_out[i].astype(BF16), g[3:4], x2d, g[4:5],
                           ple_w_gate[i].astype(BF16),
                           p[i].reshape(n_tok, p.shape[-1]),
                           ple_w_proj[i].astype(BF16))
    return x2d.reshape(batch, seq, d_model)
```

```python
import functools
import math

import jax
import jax.numpy as jnp
from jax import lax
from jax.experimental import pallas as pl
from jax.experimental.pallas import tpu as pltpu

F32 = jnp.float32
BF16 = jnp.bfloat16

GRID_W = 64
NA_HEADS = 16
NA_KH = 8
NA_KW = 16
DA_HEADS = 8
T5_BUCKETS = 32
T5_MAX_DIST = 128
EPS = 1e-6
N_MIXERS = 2

V7X_LANES = 128
V7X_VMEM_LIMIT_BYTES = 56 * 1024 * 1024

NEG_BIG = -1e30
LOG2E = math.log2(math.e)

ROW_TILE = 1024
FFN_IN_ROW_TILE = 1024
FFN_OUT_ROW_TILE = 512
OUT_PROJ_ROW_TILE = 1024
CONV_HALO = 16
FFN_CHUNK = 256
DA_TQ = 256
DA_TK = 256
DA_SUM_ROWS = 16
DA_TILES_PER_TRIP = 8
NA_ROW_GROUP = 32


def _rms(x, g):
    ms = jnp.mean(x * x, axis=-1, keepdims=True)
    return x * lax.rsqrt(ms + EPS) * g


def _params(*sem):
    return pltpu.CompilerParams(dimension_semantics=sem,
                                vmem_limit_bytes=V7X_VMEM_LIMIT_BYTES)


def _resident(shape):
    nd = len(shape)
    return pl.BlockSpec(shape, lambda *_: (0,) * nd, pipeline_mode=pl.Buffered(1))


def _toeplitz(v, n_rows, n_cols):
    length = n_rows + n_cols - 1
    lead = v.shape[:-1]
    flat = jnp.tile(v, (1,) * len(lead) + (n_rows + 1,))[..., :n_rows * (length + 1)]
    hankel = flat.reshape(lead + (n_rows, length + 1))[..., :n_cols]
    return hankel[..., ::-1]


def _norm_matmul_kernel(x_ref, g_ref, w_ref, o_ref, *, col_chunk):
    h = _rms(x_ref[...], g_ref[...]).astype(BF16)
    for c in range(o_ref.shape[1] // col_chunk):
        sl = slice(c * col_chunk, (c + 1) * col_chunk)
        o_ref[:, sl] = jnp.dot(h, w_ref[:, sl],
                               preferred_element_type=F32).astype(o_ref.dtype)


def _norm_matmul(x2d, g, w):
    n, d = x2d.shape
    nout = w.shape[1]
    return pl.pallas_call(
        functools.partial(_norm_matmul_kernel, col_chunk=1024),
        grid=(n // ROW_TILE,),
        in_specs=[pl.BlockSpec((ROW_TILE, d), lambda i: (i, 0)),
                  _resident((1, d)),
                  _resident((d, nout))],
        out_specs=pl.BlockSpec((ROW_TILE, nout), lambda i: (i, 0)),
        out_shape=jax.ShapeDtypeStruct((n, nout), BF16),
        compiler_params=_params("parallel"),
        name="norm_matmul",
    )(x2d, g, w)


def _da_proj_kernel(x_ref, g_ref, wk_ref, wqvt_ref, k_ref, qvt_ref):
    h = _rms(x_ref[...], g_ref[...]).astype(BF16)
    k_ref[...] = jnp.dot(h, wk_ref[...], preferred_element_type=F32).astype(k_ref.dtype)
    qvt_ref[0] = lax.dot_general(wqvt_ref[...], h, (((1,), (1,)), ((), ())),
                                 preferred_element_type=F32).astype(qvt_ref.dtype)


def _da_proj(x2d, g, wk, wqvt, batch, seq):
    n, d = x2d.shape
    tps = seq // ROW_TILE
    return pl.pallas_call(
        _da_proj_kernel,
        grid=(n // ROW_TILE,),
        in_specs=[pl.BlockSpec((ROW_TILE, d), lambda i: (i, 0)),
                  _resident((1, d)),
                  _resident(wk.shape),
                  _resident(wqvt.shape)],
        out_specs=[pl.BlockSpec((ROW_TILE, wk.shape[1]), lambda i: (i, 0)),
                   pl.BlockSpec((1, wqvt.shape[0], ROW_TILE), lambda i: (i // tps, 0, i % tps))],
        out_shape=[jax.ShapeDtypeStruct((n, wk.shape[1]), BF16),
                   jax.ShapeDtypeStruct((batch, wqvt.shape[0], seq), BF16)],
        compiler_params=_params("parallel"),
        name="da_proj",
    )(x2d, g, wk, wqvt)


def _matmul_norm_res_kernel(a_ref, w_ref, g_ref, x_ref, o_ref):
    f = jnp.dot(a_ref[...], w_ref[...], preferred_element_type=F32)
    o_ref[...] = x_ref[...] + _rms(f, g_ref[...])


def _matmul_norm_res(a2d, w, g, x2d):
    n, d = x2d.shape
    k = a2d.shape[1]
    tm = OUT_PROJ_ROW_TILE
    return pl.pallas_call(
        _matmul_norm_res_kernel,
        grid=(n // tm,),
        in_specs=[pl.BlockSpec((tm, k), lambda i: (i, 0)),
                  _resident((k, d)),
                  _resident((1, d)),
                  pl.BlockSpec((tm, d), lambda i: (i, 0))],
        out_specs=pl.BlockSpec((tm, d), lambda i: (i, 0)),
        out_shape=jax.ShapeDtypeStruct((n, d), F32),
        compiler_params=_params("parallel"),
        name="matmul_norm_res",
    )(a2d, w, g, x2d)


def _gelu_tanh(x):
    a = -2.0 * math.sqrt(2.0 / math.pi) * LOG2E
    b = a * 0.044715
    return x / (1.0 + jnp.exp2(x * (a + b * (x * x))))


def _ffn_in_kernel(xp_ref, x_ref, xn_ref, g_ref, wg_ref, wv_ref, cw_ref, cb_ref,
                   og_ref, ov_ref, gext_ref, *, tiles_per_seq):
    i = pl.program_id(0)
    t = i % tiles_per_seq
    g = g_ref[...]
    has_prev = (t > 0).astype(F32)
    has_next = (t < tiles_per_seq - 1).astype(F32)
    h_prev = (_rms(xp_ref[...], g) * has_prev).astype(BF16)
    h_main = _rms(x_ref[...], g).astype(BF16)
    h_next = (_rms(xn_ref[...], g) * has_next).astype(BF16)
    h_ext = jnp.concatenate([h_prev, h_main, h_next], axis=0)
    tm = x_ref.shape[0]
    n_chunks = og_ref.shape[1] // FFN_CHUNK

    def project(c):
        sl = slice(c * FFN_CHUNK, (c + 1) * FFN_CHUNK)
        gext_ref[c % 2] = jnp.dot(h_ext, wg_ref[:, sl], preferred_element_type=F32)
        ov_ref[:, sl] = jnp.dot(h_main, wv_ref[:, sl],
                                preferred_element_type=F32).astype(ov_ref.dtype)

    def activate(c):
        sl = slice(c * FFN_CHUNK, (c + 1) * FFN_CHUNK)
        gext = gext_ref[c % 2]
        cw = cw_ref[:, sl]
        rows = gext.shape[0]
        g_prev = pltpu.roll(gext, 1, axis=0)[CONV_HALO:CONV_HALO + tm]
        g_next = pltpu.roll(gext, rows - 1, axis=0)[CONV_HALO:CONV_HALO + tm]
        gate = (g_prev * cw[0:1] + gext[CONV_HALO:CONV_HALO + tm] * cw[1:2]
                + g_next * cw[2:3] + cb_ref[:, sl])
        og_ref[:, sl] = gate.astype(og_ref.dtype)

    project(0)
    for c in range(n_chunks):
        if c + 1 < n_chunks:
            project(c + 1)
        activate(c)


def _ffn_in(x2d, g, w_in, conv_w, conv_b, seq_len):
    n, d = x2d.shape
    nf = w_in.shape[1] // 2
    tm = FFN_IN_ROW_TILE
    hb = tm // CONV_HALO
    last = n // CONV_HALO - 1
    return pl.pallas_call(
        functools.partial(_ffn_in_kernel, tiles_per_seq=seq_len // tm),
        grid=(n // tm,),
        in_specs=[
            pl.BlockSpec((CONV_HALO, d), lambda i: (jnp.maximum(i * hb - 1, 0), 0)),
            pl.BlockSpec((tm, d), lambda i: (i, 0)),
            pl.BlockSpec((CONV_HALO, d), lambda i: (jnp.minimum((i + 1) * hb, last), 0)),
            _resident((1, d)),
            pl.BlockSpec((d, nf), lambda i: (0, 0)),
            pl.BlockSpec((d, nf), lambda i: (0, 1)),
            _resident((3, nf)),
            _resident((1, nf)),
        ],
        out_specs=[pl.BlockSpec((tm, nf), lambda i: (i, 0)),
                   pl.BlockSpec((tm, nf), lambda i: (i, 0))],
        out_shape=[jax.ShapeDtypeStruct((n, nf), BF16),
                   jax.ShapeDtypeStruct((n, nf), BF16)],
        scratch_shapes=[pltpu.VMEM((2, tm + 2 * CONV_HALO, FFN_CHUNK), F32)],
        compiler_params=_params("parallel"),
        name="ffn_in",
    )(x2d, x2d, x2d, g, w_in, w_in, conv_w, conv_b)


def _ffn_out_ple_kernel(gate_ref, val_ref, wo_ref, g3_ref, x_ref, g4_ref, wgate_ref, p_ref,
                        wproj_ref, o_ref):
    a = (_gelu_tanh(gate_ref[...].astype(F32)) * val_ref[...].astype(F32)).astype(BF16)
    f = jnp.dot(a, wo_ref[...], preferred_element_type=F32)
    x2 = x_ref[...] + _rms(f, g3_ref[...])
    hg = _rms(x2, g4_ref[...]).astype(BF16)
    gate = jax.nn.sigmoid(jnp.dot(hg, wgate_ref[...], preferred_element_type=F32))
    emb = jnp.dot(p_ref[...].astype(BF16), wproj_ref[...], preferred_element_type=F32)
    o_ref[...] = x2 + gate * emb


def _ffn_out_ple(gate2d, val2d, w_out, g3, x2d, g4, w_gate, p2d, w_proj):
    n, d = x2d.shape
    nf = gate2d.shape[1]
    pd = p2d.shape[1]
    tm = FFN_OUT_ROW_TILE
    return pl.pallas_call(
        _ffn_out_ple_kernel,
        grid=(n // tm,),
        in_specs=[pl.BlockSpec((tm, nf), lambda i: (i, 0)),
                  pl.BlockSpec((tm, nf), lambda i: (i, 0)),
                  _resident((nf, d)),
                  _resident((1, d)),
                  pl.BlockSpec((tm, d), lambda i: (i, 0)),
                  _resident((1, d)),
                  _resident((d, d)),
                  pl.BlockSpec((tm, pd), lambda i: (i, 0)),
                  _resident((pd, d))],
        out_specs=pl.BlockSpec((tm, d), lambda i: (i, 0)),
        out_shape=jax.ShapeDtypeStruct((n, d), F32),
        compiler_params=_params("parallel"),
        name="ffn_out_ple",
    )(gate2d, val2d, w_out, g3, x2d, g4, w_gate, p2d, w_proj)


def _na_kernel(q_ref, k_ref, v_ref, bias_ref, o_ref, s_ref, p_ref, *, rows):
    w = GRID_W
    nkeys = NA_KH * w
    group = s_ref.shape[0]
    lane = lax.broadcasted_iota(jnp.int32, (w, V7X_LANES), 1)
    lo = lane < (V7X_LANES // 2)

    def group_fn(gi, carry):
        geo = []
        for i in range(group):
            r = gi * group + i
            r0 = jnp.clip(r - NA_KH // 2, 0, rows - NA_KH)
            geo.append((pl.multiple_of(r * w, w), pl.multiple_of(r0 * w, w), r - r0))
        for i, (qoff, koff, _) in enumerate(geo):
            q = q_ref[0, pl.ds(qoff, w), :]
            zero = jnp.zeros_like(q)
            qs = jnp.concatenate([jnp.where(lo, q, zero), jnp.where(lo, zero, q)], axis=0)
            s_ref[i] = lax.dot_general(qs, k_ref[0, pl.ds(koff, nkeys), :],
                                       (((1,), (1,)), ((), ())),
                                       preferred_element_type=F32)
        sums = []
        for i, (_, _, d) in enumerate(geo):
            first = NA_KH - 1 - d
            bias = [jnp.concatenate([bias_ref[0, hh, first + e] for e in range(0, NA_KH, 2)],
                                    axis=1) for hh in range(2)]
            s = s_ref[i] + jnp.concatenate(bias, axis=0)
            p = jnp.exp2(s - jnp.max(s, axis=-1, keepdims=True))
            sums.append(jnp.sum(p, axis=-1, keepdims=True))
            p_ref[i] = p.astype(BF16)
        for i, (qoff, koff, _) in enumerate(geo):
            o = jnp.dot(p_ref[i], v_ref[0, pl.ds(koff, nkeys), :],
                        preferred_element_type=F32) / sums[i]
            o_ref[0, pl.ds(qoff, w), :] = jnp.where(lo, o[:w], o[w:]).astype(o_ref.dtype)
        return carry

    lax.fori_loop(0, rows // group, group_fn, 0)


def _na_bias_table(rpb):
    h = rpb.shape[0]
    w = GRID_W
    pad = w - NA_KW
    vp = jnp.pad(rpb.astype(F32), ((0, 0), (0, 0), (pad, pad)))[..., ::-1]
    tt = _toeplitz(vp, w, w)
    c = jnp.arange(w)
    kc = jnp.arange(w)
    c0 = jnp.clip(c - NA_KW // 2, 0, w - NA_KW)
    valid = (kc[None, :] >= c0[:, None]) & (kc[None, :] < c0[:, None] + NA_KW)
    tt = jnp.where(valid, tt * LOG2E, NEG_BIG)
    pairs = jnp.concatenate([tt[:, :-1], tt[:, 1:]], axis=-1)
    return pairs.reshape(h // 2, 2, 2 * NA_KH - 2, w, 2 * w)


def _neighborhood_attention(qkv, bias, batch, seq, d_model):
    hp = NA_HEADS // 2
    rows = seq // GRID_W
    blk = (1, seq, V7X_LANES)
    return pl.pallas_call(
        functools.partial(_na_kernel, rows=rows),
        grid=(hp, batch),
        in_specs=[pl.BlockSpec(blk, lambda h, b: (b, 0, h)),
                  pl.BlockSpec(blk, lambda h, b: (b, 0, hp + h)),
                  pl.BlockSpec(blk, lambda h, b: (b, 0, 2 * hp + h)),
                  pl.BlockSpec((1,) + bias.shape[1:], lambda h, b: (h, 0, 0, 0, 0))],
        out_specs=pl.BlockSpec(blk, lambda h, b: (b, 0, h)),
        out_shape=jax.ShapeDtypeStruct((batch, seq, d_model), BF16),
        scratch_shapes=[pltpu.VMEM((NA_ROW_GROUP, 2 * GRID_W, NA_KH * GRID_W), F32),
                        pltpu.VMEM((NA_ROW_GROUP, 2 * GRID_W, NA_KH * GRID_W), BF16)],
        compiler_params=_params("parallel", "parallel"),
        name="neighborhood_attention",
    )(qkv, qkv, qkv, bias)


def _t5_bucket(rel):
    half = T5_BUCKETS // 2
    max_exact = half // 2
    sign_off = jnp.where(rel > 0, half, 0)
    n = jnp.abs(rel)
    nf = jnp.maximum(n, 1).astype(F32)
    large = max_exact + (jnp.log(nf / max_exact) / math.log(T5_MAX_DIST / max_exact)
                         * (half - max_exact)).astype(jnp.int32)
    large = jnp.minimum(large, half - 1)
    return sign_off + jnp.where(n < max_exact, n, large)


DA_BLK = V7X_LANES
DA_BLK_SPAN = 2
assert (DA_BLK_SPAN - 1) * DA_BLK + 1 >= T5_MAX_DIST


def _da_bias_blocks(t5_table):
    span = DA_BLK_SPAN * DA_BLK + DA_BLK - 1
    rel = jnp.arange(-span, span + 1)
    vals = (t5_table[_t5_bucket(rel)].astype(F32) * LOG2E).T
    vecs = jnp.stack([lax.slice_in_dim(vals, DA_BLK * d - (DA_BLK - 1) + span,
                                       DA_BLK * d + DA_BLK + span, axis=1)
                      for d in range(-DA_BLK_SPAN, DA_BLK_SPAN + 1)], axis=1)
    blocks = _toeplitz(vecs, DA_BLK, DA_BLK)
    far = jnp.stack([vals[:, 0], vals[:, -1]], axis=1)
    return blocks, far


def _da_kernel(far_ref, qt_ref, k_ref, vt_ref, blk_ref, lam_ref, g_ref, o_ref,
               qst_ref, vte_ref, st_ref, p_ref, m_ref, alpha_ref, acc_ref,
               *, lambda_init, n_kt, n_qt):
    h = pl.program_id(0)
    tq, tk = DA_TQ, DA_TK
    nv = V7X_LANES
    half = V7X_LANES // 2
    c_left = far_ref[h, 0]
    c_right = far_ref[h, 1]

    vte_ref[0:nv, :] = vt_ref[0]
    vte_ref[nv:, :] = jnp.ones((DA_SUM_ROWS, vte_ref.shape[1]), BF16)

    lam = lam_ref[...]
    lam_full = (jnp.exp(jnp.sum(lam[0:1] * lam[1:2], axis=-1, keepdims=True))
                - jnp.exp(jnp.sum(lam[2:3] * lam[3:4], axis=-1, keepdims=True))
                + lambda_init)

    def first_tile(qi):
        return (qi * tq + tk - (T5_MAX_DIST - 1)) // tk - 1

    n_band = max((q0 + tq - 1 + T5_MAX_DIST - 1) // tk - (q0 - (T5_MAX_DIST - 1)) // tk + 1
                 for q0 in range(0, n_qt * tq, tq))

    def tile_of(qi, slot):
        return (first_tile(qi) + slot) & (n_kt - 1)

    def scores(qi, slot, bias=None):
        off = pl.multiple_of(tile_of(qi, slot) * tk, tk)
        st = jnp.dot(k_ref[0, pl.ds(off, tk), :], qst_ref[...], preferred_element_type=F32)
        if bias is not None:
            st = st + jnp.concatenate([bias, bias], axis=1)
        st_ref[slot % 2] = st

    def accumulate(qi, slot):
        off = pl.multiple_of(tile_of(qi, slot) * tk, tk)
        acc_ref[...] = alpha_ref[...] * acc_ref[...] + jnp.dot(
            vte_ref[:, pl.ds(off, tk)], p_ref[slot % 2], preferred_element_type=F32)

    def softmax(slot, c):
        st = st_ref[slot % 2]
        m_prev = m_ref[...]
        m_next = jnp.maximum(m_prev, jnp.max(st, axis=0, keepdims=True) + c)
        p_ref[slot % 2] = jnp.exp2(st - (m_next - c)).astype(BF16)
        alpha_ref[...] = jnp.exp2(m_prev - m_next)
        m_ref[...] = m_next

    def full_bias(qi, slot):
        j = tile_of(qi, slot)
        rows = []
        for kb in range(tk // DA_BLK):
            d = [(tk // DA_BLK) * j + kb - (tq // DA_BLK) * qi - qb
                 for qb in range(tq // DA_BLK)]
            rows.append(jnp.concatenate(
                [blk_ref[0, jnp.clip(x, -DA_BLK_SPAN, DA_BLK_SPAN) + DA_BLK_SPAN] for x in d],
                axis=1))
        return jnp.concatenate(rows, axis=0)

    def start(qi):
        feat = lax.broadcasted_iota(jnp.int32, (V7X_LANES, tq), 0)
        qt = qt_ref[0, :, pl.ds(pl.multiple_of(qi * tq, tq), tq)]
        zero = jnp.zeros_like(qt)
        qst_ref[:, 0:tq] = jnp.where(feat < half, qt, zero)
        qst_ref[:, tq:2 * tq] = jnp.where(feat < half, zero, qt)
        scores(qi, 0, full_bias(qi, 0))

    def run_slots(qi):
        m_ref[...] = jnp.full(m_ref.shape, NEG_BIG, F32)
        alpha_ref[...] = jnp.zeros(alpha_ref.shape, F32)
        acc_ref[...] = jnp.zeros(acc_ref.shape, F32)
        jn = first_tile(qi)

        def far_c(slot):
            return jnp.where(tile_of(qi, slot) > jn, c_right, c_left)

        def bias_of(slot):
            return full_bias(qi, slot) if slot < n_band else None

        scores(qi, 1, bias_of(1))
        softmax(0, 0.0)
        for s in range(1, n_kt - 1):
            scores(qi, s + 1, bias_of(s + 1))
            accumulate(qi, s - 1)
            softmax(s, 0.0 if s < n_band else far_c(s))
        accumulate(qi, n_kt - 2)
        softmax(n_kt - 1, far_c(n_kt - 1))
        accumulate(qi, n_kt - 1)

    def finish(qi):
        o_all = acc_ref[0:nv, :] / acc_ref[nv:nv + 1, :]
        o_t = o_all[:, :tq] - lam_full * o_all[:, tq:]
        o = _rms(o_t.T, g_ref[...]) * (1.0 - lambda_init)
        o_ref[0, pl.ds(pl.multiple_of(qi * tq, tq), tq), :] = o.astype(o_ref.dtype)

    def q_group(gi, carry):
        first = gi * DA_TILES_PER_TRIP
        start(first)
        for t in range(DA_TILES_PER_TRIP):
            run_slots(first + t)
            if t + 1 < DA_TILES_PER_TRIP:
                start(first + t + 1)
            finish(first + t)
        return carry

    lax.fori_loop(0, n_qt // DA_TILES_PER_TRIP, q_group, 0)


def _diff_attention(k, qvt, blocks, far, lam, subln_g, lambda_init, batch, seq, d_model):
    nh = DA_HEADS
    n_qt = seq // DA_TQ
    n_kt = seq // DA_TK
    assert n_kt & (n_kt - 1) == 0
    return pl.pallas_call(
        functools.partial(_da_kernel, lambda_init=lambda_init, n_kt=n_kt, n_qt=n_qt),
        grid=(nh, batch),
        in_specs=[pl.BlockSpec(memory_space=pltpu.SMEM),
                  pl.BlockSpec((1, V7X_LANES, seq), lambda h, b: (b, h, 0)),
                  pl.BlockSpec((1, seq, V7X_LANES), lambda h, b: (b, 0, h)),
                  pl.BlockSpec((1, V7X_LANES, seq), lambda h, b: (b, nh + h, 0)),
                  pl.BlockSpec((1,) + blocks.shape[1:], lambda h, b: (h, 0, 0, 0)),
                  _resident(lam.shape),
                  _resident(subln_g.shape)],
        out_specs=pl.BlockSpec((1, seq, V7X_LANES), lambda h, b: (b, 0, h)),
        out_shape=jax.ShapeDtypeStruct((batch, seq, d_model), BF16),
        scratch_shapes=[pltpu.VMEM((V7X_LANES, 2 * DA_TQ), BF16),
                        pltpu.VMEM((V7X_LANES + DA_SUM_ROWS, seq), BF16),
                        pltpu.VMEM((2, DA_TK, 2 * DA_TQ), F32),
                        pltpu.VMEM((2, DA_TK, 2 * DA_TQ), BF16),
                        pltpu.VMEM((1, 2 * DA_TQ), F32),
                        pltpu.VMEM((1, 2 * DA_TQ), F32),
                        pltpu.VMEM((V7X_LANES + DA_SUM_ROWS, 2 * DA_TQ), F32)],
        compiler_params=_params("parallel", "parallel"),
        name="diff_attention",
    )(far, qvt, k, qvt, blocks, lam, subln_g)


def kernel(x, p, norm_g, na_w_qkv, na_rpb, na_w_o, da_w_qkv, da_lambda, da_subln_g,
           da_w_o, t5_table, ffn_w_in, ffn_conv_w, ffn_conv_b, ffn_w_out,
           ple_w_gate, ple_w_proj):
    batch, seq, d_model = x.shape
    depth = norm_g.shape[0]
    n_tok = batch * seq
    x2d = x.reshape(n_tok, d_model)

    for i in range(depth):
        g = norm_g[i].astype(F32)
        j = i // N_MIXERS
        if i % N_MIXERS == 0:
            w = na_w_qkv[j]
            dh = d_model // NA_HEADS
            w = jnp.concatenate([w[:, :d_model] * (LOG2E * dh ** -0.5), w[:, d_model:]], axis=1)
            qkv = _norm_matmul(x2d, g[0:1], w.astype(BF16))
            att = _neighborhood_attention(qkv.reshape(batch, seq, 3 * d_model),
                                          _na_bias_table(na_rpb[j]), batch, seq, d_model)
            w_o = na_w_o[j]
        else:
            lambda_init = 0.8 - 0.6 * math.exp(-0.3 * i)
            w = da_w_qkv[j]
            dh = d_model // (2 * DA_HEADS)
            wqvt = jnp.concatenate([w[:, :d_model] * (LOG2E * dh ** -0.5),
                                    w[:, 2 * d_model:]], axis=1).T.astype(BF16)
            k, qvt = _da_proj(x2d, g[0:1], w[:, d_model:2 * d_model].astype(BF16), wqvt,
                              batch, seq)
            blocks, far = _da_bias_blocks(t5_table)
            att = _diff_attention(k.reshape(batch, seq, d_model), qvt, blocks, far,
                                  da_lambda[j].astype(F32),
                                  da_subln_g[j].astype(F32)[None, :],
                                  lambda_init, batch, seq, d_model)
            w_o = da_w_o[j]
        x2d = _matmul_norm_res(att.reshape(n_tok, d_model), w_o.astype(BF16), g[1:2], x2d)
        w_in = ffn_w_in[i].astype(BF16)
        gate, val = _ffn_in(x2d, g[2:3], w_in,
                      ffn_conv_w[i].astype(F32), ffn_conv_b[i].astype(F32)[None, :], seq)
        x2d = _ffn_out_ple(gate, val, ffn_w_out[i].astype(BF16), g[3:4], x2d, g[4:5],
                           ple_w_gate[i].astype(BF16),
                           p[i].reshape(n_tok, p.shape[-1]),
                           ple_w_proj[i].astype(BF16))
    return x2d.reshape(batch, seq, d_model)
```
